```python
import jax, jax.numpy as jnp
from jax import lax
import numpy as np

D_MODEL = 1024
BATCH = 8
SEQ = 4096
DEPTH = 2

CHUNK = 64
MIX_WIDTH = D_MODEL
ATT_HEADS = 8
ATT_HEAD_DIM = 64
ATT_WIDTH = ATT_HEADS * ATT_HEAD_DIM
LEFT_CHUNKS = 8
BAND = (LEFT_CHUNKS + 1) * CHUNK
REL_CLIP = 256
RET_HEADS = 4
RET_HEAD_DIM = 128
RET_WIDTH = RET_HEADS * RET_HEAD_DIM
SPLIT_WIDTHS = (ATT_WIDTH, ATT_WIDTH, ATT_WIDTH, ATT_WIDTH,
                RET_WIDTH, RET_WIDTH, RET_WIDTH, RET_WIDTH)
IN_WIDTH = sum(SPLIT_WIDTHS)
EPS = 1e-6
ROPE_BASE = 10000.0
NEG_INF = -1e30

kernel_name = "hybrid_chunked_attn_retention"


def rms_norm(x, gain=None):
    xf = x.astype(jnp.float32)
    y = xf * lax.rsqrt(jnp.mean(xf * xf, axis=-1, keepdims=True) + EPS)
    if gain is not None:
        y = y * gain.astype(jnp.float32)
    return y.astype(x.dtype)


def rotary_tables(positions, dim):
    inv_freq = 1.0 / (ROPE_BASE ** jnp.linspace(0.0, 1.0, dim // 2, dtype=jnp.float32))
    ang = positions.astype(jnp.float32)[:, None] * inv_freq[None, :]
    return jnp.cos(ang), jnp.sin(ang)


def apply_rotary(t, cos, sin):
    t1, t2 = jnp.split(t, 2, axis=-1)
    c = cos[None, :, None, :]
    s = sin[None, :, None, :]
    return jnp.concatenate([t1 * c - t2 * s, t1 * s + t2 * c], axis=-1)


def chunked_rel_attention(q, k, v, rel_bias):
    b, s, h, dh = q.shape
    nc = s // CHUNK
    qc = q.astype(jnp.float32).reshape(b, nc, CHUNK, h, dh)
    pad = ((0, 0), (LEFT_CHUNKS * CHUNK, 0), (0, 0), (0, 0))
    kp = jnp.pad(k.astype(jnp.float32), pad).reshape(b, nc + LEFT_CHUNKS, CHUNK, h, dh)
    vp = jnp.pad(v.astype(jnp.float32), pad).reshape(b, nc + LEFT_CHUNKS, CHUNK, h, dh)
    kb = jnp.concatenate([kp[:, j:j + nc] for j in range(LEFT_CHUNKS + 1)], axis=2)
    vb = jnp.concatenate([vp[:, j:j + nc] for j in range(LEFT_CHUNKS + 1)], axis=2)
    scores = jnp.einsum("bnqhd,bnkhd->bnhqk", qc, kb) * (dh ** -0.5)
    q_off = jnp.arange(CHUNK)[:, None] + LEFT_CHUNKS * CHUNK
    k_off = jnp.arange(BAND)[None, :]
    dist = q_off - k_off
    bias = rel_bias.astype(jnp.float32)[:, jnp.clip(dist, -REL_CLIP, REL_CLIP) + REL_CLIP]
    key_pos = (jnp.arange(nc)[:, None] - LEFT_CHUNKS) * CHUNK + k_off
    valid = (key_pos >= 0)[None, :, None, None, :]
    scores = jnp.where(valid, scores + bias[None, None], NEG_INF)
    p = jax.nn.softmax(scores, axis=-1)
    out = jnp.einsum("bnhqk,bnkhd->bnqhd", p, vb)
    return out.reshape(b, s, h * dh)


def chunkwise_retention(q, k, v, cos, sin):
    b, s, h, dk = q.shape
    dv = v.shape[-1]
    nc = s // CHUNK
    q = apply_rotary(q.astype(jnp.float32), cos, sin)
    k = apply_rotary(k.astype(jnp.float32), cos, sin) * (dk ** -0.5)
    v = v.astype(jnp.float32)
    log_g = jnp.log1p(-jnp.exp2(-5.0 - jnp.arange(h, dtype=jnp.float32)))
    idx = jnp.arange(CHUNK, dtype=jnp.float32)
    intra_decay = jnp.exp(jnp.abs(idx[:, None] - idx[None, :])[None] * log_g[:, None, None])
    k_decay = jnp.exp((CHUNK - 1 - idx)[None, :] * log_g[:, None])
    q_decay = jnp.exp((idx + 1.0)[None, :] * log_g[:, None])
    chunk_decay = jnp.exp(CHUNK * log_g)
    qc = q.reshape(b, nc, CHUNK, h, dk)
    kc = k.reshape(b, nc, CHUNK, h, dk)
    vc = v.reshape(b, nc, CHUNK, h, dv)
    att = jnp.einsum("bnihd,bnjhd->bnhij", qc, kc) * intra_decay[None, None]
    o_intra = jnp.einsum("bnhij,bnjhe->bnihe", att, vc)
    kv = jnp.einsum("bnjhd,hj,bnjhe->bnhde", kc, k_decay, vc)

    def step(state, kv_n):
        return state * chunk_decay[None, :, None, None] + kv_n, state

    _, states = lax.scan(step, jnp.zeros((b, h, dk, dv), jnp.float32), jnp.moveaxis(kv, 1, 0))
    states = jnp.moveaxis(states, 0, 1)
    o_inter = jnp.einsum("bnihd,hi,bnhde->bnihe", qc, q_decay, states)
    o = rms_norm(o_intra + o_inter)
    return o.reshape(b, s, h * dv)


def hybrid_layer(x, norm_gain, w_in, w_out, rel_bias, cos, sin):
    b, s, _ = x.shape
    hn = rms_norm(x, norm_gain)
    proj = hn @ w_in.astype(hn.dtype)
    cuts = list(np.cumsum(SPLIT_WIDTHS)[:-1])
    aq, ak, av, ag, rq, rk, rv, rg = jnp.split(proj, cuts, axis=-1)
    ya = chunked_rel_attention(aq.reshape(b, s, ATT_HEADS, ATT_HEAD_DIM),
                               ak.reshape(b, s, ATT_HEADS, ATT_HEAD_DIM),
                               av.reshape(b, s, ATT_HEADS, ATT_HEAD_DIM), rel_bias)
    yr = chunkwise_retention(rq.reshape(b, s, RET_HEADS, RET_HEAD_DIM),
                             rk.reshape(b, s, RET_HEADS, RET_HEAD_DIM),
                             rv.reshape(b, s, RET_HEADS, RET_HEAD_DIM), cos, sin)
    y = jnp.concatenate([ya * jax.nn.silu(ag.astype(jnp.float32)),
                         yr * jax.nn.silu(rg.astype(jnp.float32))], axis=-1).astype(x.dtype)
    return x + y @ w_out.astype(x.dtype)


def setup_inputs(seed: int = 0) -> dict:
    key = jax.random.key(seed)
    kx, kg, ki, ko, kb, kf = jax.random.split(key, 6)
    x = jax.random.normal(kx, (BATCH, SEQ, D_MODEL), jnp.float32)
    positions = jnp.arange(SEQ, dtype=jnp.int32)
    norm_gain = 1.0 + 0.02 * jax.random.normal(kg, (DEPTH, D_MODEL), jnp.float32)
    w_in = jax.random.normal(ki, (DEPTH, D_MODEL, IN_WIDTH), jnp.float32) * D_MODEL ** -0.5
    w_out = jax.random.normal(ko, (DEPTH, MIX_WIDTH, D_MODEL), jnp.float32) * MIX_WIDTH ** -0.5
    rel_bias = 0.1 * jax.random.normal(kb, (DEPTH, ATT_HEADS, 2 * REL_CLIP + 1), jnp.float32)
    final_gain = 1.0 + 0.02 * jax.random.normal(kf, (D_MODEL,), jnp.float32)
    return {"x": x, "positions": positions, "norm_gain": norm_gain, "w_in": w_in,
            "w_out": w_out, "rel_bias": rel_bias, "final_gain": final_gain}


def reference(x, positions, norm_gain, w_in, w_out, rel_bias, final_gain):
    cos, sin = rotary_tables(positions, RET_HEAD_DIM)
    for layer in range(DEPTH):
        x = hybrid_layer(x, norm_gain[layer], w_in[layer], w_out[layer], rel_bias[layer], cos, sin)
    return rms_norm(x, final_gain)
```

```python
import functools

import jax
import jax.numpy as jnp
from jax import lax
from jax.experimental import pallas as pl
from jax.experimental.pallas import tpu as pltpu

D_MODEL = 1024
CHUNK = 64
ATT_HEADS = 8
ATT_HEAD_DIM = 64
ATT_WIDTH = ATT_HEADS * ATT_HEAD_DIM
LEFT_CHUNKS = 8
REL_CLIP = 256
RET_HEADS = 4
RET_HEAD_DIM = 128
RET_WIDTH = RET_HEADS * RET_HEAD_DIM
IN_WIDTH = 4 * ATT_WIDTH + 4 * RET_WIDTH
MIX_WIDTH = ATT_WIDTH + RET_WIDTH
EPS = 1e-6
ROPE_BASE = 10000.0
NEG_INF = -1e30

LANES = 128
SEQ_BLOCK = 512
HIST = LEFT_CHUNKS * CHUNK
ATT_QBLOCK = 128
ATT_KBLOCK = HIST + ATT_QBLOCK
RET_BLOCK = 128
VMEM_LIMIT_BYTES = 56 * 1024 * 1024

BF16 = jnp.bfloat16
F32 = jnp.float32


def _dot(a, b):
    return jnp.dot(a, b, preferred_element_type=F32)


def _dot_nt(a, b):
    return lax.dot_general(a, b, (((1,), (1,)), ((), ())), preferred_element_type=F32)


def _dot_tn(a, b):
    return lax.dot_general(a, b, (((0,), (0,)), ((), ())), preferred_element_type=F32)


def _silu(g):
    return g * (1.0 / (1.0 + jnp.exp(-g)))


def _layer_kernel(x_ref, gain_ref, win_ref, wout_ref, bias_ref, cos_ref, sin_ref,
                  dmat_ref, qdec_ref, kdec_ref, bdec_ref, fgain_ref,
                  out_ref,
                  hn_s, q_s, kh_s, vh_s, ga_s, rq_s, rk_s, rkd_s, rv_s, gr_s, y_s, state_s,
                  *, final):
    s = pl.program_id(1)

    @pl.when(s == 0)
    def _():
        kh_s[0:HIST, :] = jnp.zeros((HIST, ATT_WIDTH), BF16)
        vh_s[0:HIST, :] = jnp.zeros((HIST, ATT_WIDTH), BF16)
        state_s[...] = jnp.zeros(state_s.shape, F32)

    gain = gain_ref[...]
    rows = 64

    def norm_body(i, c):
        r0 = pl.multiple_of(i * rows, rows)
        xb = x_ref[0, pl.ds(r0, rows), :]
        ms = jnp.mean(xb * xb, axis=-1, keepdims=True)
        hn_s[pl.ds(r0, rows), :] = (xb * lax.rsqrt(ms + EPS) * gain).astype(BF16)
        return c

    lax.fori_loop(0, SEQ_BLOCK // rows, norm_body, 0)

    hn = hn_s[...]

    def piece(c):
        return _dot(hn, win_ref[:, c * ATT_WIDTH:(c + 1) * ATT_WIDTH])

    q_s[...] = (piece(0) * (ATT_HEAD_DIM ** -0.5)).astype(BF16)
    kh_s[HIST:HIST + SEQ_BLOCK, :] = piece(1).astype(BF16)
    vh_s[HIST:HIST + SEQ_BLOCK, :] = piece(2).astype(BF16)
    ga_s[...] = _silu(piece(3))

    cos = cos_ref[...]
    sin = sin_ref[...]

    def rotary(t):
        return t * cos + pltpu.roll(t, RET_HEAD_DIM // 2, 1) * sin

    rq = piece(4)
    rk = piece(5)
    n_rb = SEQ_BLOCK // RET_BLOCK
    for h in range(RET_HEADS):
        cs = slice(h * RET_HEAD_DIM, (h + 1) * RET_HEAD_DIM)
        rq_s[:, cs] = rotary(rq[:, cs]).astype(BF16)
        kr = rotary(rk[:, cs]) * (RET_HEAD_DIM ** -0.5)
        rk_s[:, cs] = kr.astype(BF16)
        kdec = jnp.concatenate([kdec_ref[h]] * n_rb, axis=0)
        rkd_s[:, cs] = (kr * kdec).astype(BF16)
    rv_s[...] = piece(6).astype(BF16)
    gr_s[...] = _silu(piece(7))

    lane = lax.broadcasted_iota(jnp.int32, (ATT_QBLOCK, LANES), 1)
    low_half = lane < ATT_HEAD_DIM
    key_idx = lax.broadcasted_iota(jnp.int32, (1, ATT_KBLOCK), 1)

    def attn_body(qi, c):
        r0 = pl.multiple_of(qi * ATT_QBLOCK, ATT_QBLOCK)
        start_mask = jnp.where(jnp.logical_and(s == 0, key_idx + r0 < HIST), NEG_INF, 0.0)
        for hp in range(ATT_HEADS // 2):
            cs = slice(hp * LANES, (hp + 1) * LANES)
            qp = q_s[pl.ds(r0, ATT_QBLOCK), cs]
            kp = kh_s[pl.ds(r0, ATT_KBLOCK), cs]
            vp = vh_s[pl.ds(r0, ATT_KBLOCK), cs]
            outs = []
            for sub in range(2):
                h = 2 * hp + sub
                keep = low_half if sub == 0 else jnp.logical_not(low_half)
                qm = jnp.where(keep, qp, jnp.zeros_like(qp))
                sc = _dot_nt(qm, kp) + (bias_ref[h] + start_mask)
                m = jnp.max(sc, axis=-1, keepdims=True)
                e = jnp.exp(sc - m)
                l = jnp.sum(e, axis=-1, keepdims=True)
                o = _dot(e.astype(BF16), vp)
                outs.append(o * (1.0 / l))
            o_pair = jnp.where(low_half, outs[0], outs[1])
            y_s[pl.ds(r0, ATT_QBLOCK), cs] = (o_pair * ga_s[pl.ds(r0, ATT_QBLOCK), cs]).astype(BF16)
        return c

    lax.fori_loop(0, SEQ_BLOCK // ATT_QBLOCK, attn_body, 0)

    def ret_body(ri, c):
        r0 = pl.multiple_of(ri * RET_BLOCK, RET_BLOCK)
        rsl = pl.ds(r0, RET_BLOCK)
        for h in range(RET_HEADS):
            cs = slice(h * RET_HEAD_DIM, (h + 1) * RET_HEAD_DIM)
            q = rq_s[rsl, cs]
            k = rk_s[rsl, cs]
            kd = rkd_s[rsl, cs]
            v = rv_s[rsl, cs]
            st = state_s[h]
            att = _dot_nt(q, k) * dmat_ref[h]
            o = _dot(att.astype(BF16), v) + qdec_ref[h] * _dot(q, st.astype(BF16))
            state_s[h] = st * bdec_ref[h] + _dot_tn(kd, v)
            ms = jnp.mean(o * o, axis=-1, keepdims=True)
            o = o * lax.rsqrt(ms + EPS)
            ycs = slice(ATT_WIDTH + h * RET_HEAD_DIM, ATT_WIDTH + (h + 1) * RET_HEAD_DIM)
            y_s[rsl, ycs] = (o * gr_s[rsl, cs]).astype(BF16)
        return c

    lax.fori_loop(0, SEQ_BLOCK // RET_BLOCK, ret_body, 0)

    res = x_ref[0] + _dot(y_s[...], wout_ref[...])
    if final:
        ms = jnp.mean(res * res, axis=-1, keepdims=True)
        res = res * lax.rsqrt(ms + EPS) * fgain_ref[...]
    out_ref[0] = res

    kh_s[0:HIST, :] = kh_s[SEQ_BLOCK:SEQ_BLOCK + HIST, :]
    vh_s[0:HIST, :] = vh_s[SEQ_BLOCK:SEQ_BLOCK + HIST, :]


def _const_spec(shape):
    zeros = (0,) * len(shape)
    return pl.BlockSpec(shape, lambda b, s: zeros)


def _layer(x, gain, win, wout, bias, cos, sin, dmat, qdec, kdec, bdec, fgain, *, final):
    batch, seq, _ = x.shape
    grid = (batch, seq // SEQ_BLOCK)
    blk = pl.BlockSpec((1, SEQ_BLOCK, D_MODEL), lambda b, s: (b, s, 0))
    tab = pl.BlockSpec((SEQ_BLOCK, LANES), lambda b, s: (s, 0))
    in_specs = [
        blk,
        _const_spec(gain.shape), _const_spec(win.shape), _const_spec(wout.shape),
        _const_spec(bias.shape), tab, tab,
        _const_spec(dmat.shape), _const_spec(qdec.shape), _const_spec(kdec.shape),
        _const_spec(bdec.shape), _const_spec(fgain.shape),
    ]
    scratch = [
        pltpu.VMEM((SEQ_BLOCK, D_MODEL), BF16),
        pltpu.VMEM((SEQ_BLOCK, ATT_WIDTH), BF16),
        pltpu.VMEM((HIST + SEQ_BLOCK, ATT_WIDTH), BF16),
        pltpu.VMEM((HIST + SEQ_BLOCK, ATT_WIDTH), BF16),
        pltpu.VMEM((SEQ_BLOCK, ATT_WIDTH), F32),
        pltpu.VMEM((SEQ_BLOCK, RET_WIDTH), BF16),
        pltpu.VMEM((SEQ_BLOCK, RET_WIDTH), BF16),
        pltpu.VMEM((SEQ_BLOCK, RET_WIDTH), BF16),
        pltpu.VMEM((SEQ_BLOCK, RET_WIDTH), BF16),
        pltpu.VMEM((SEQ_BLOCK, RET_WIDTH), F32),
        pltpu.VMEM((SEQ_BLOCK, MIX_WIDTH), BF16),
        pltpu.VMEM((RET_HEADS, RET_HEAD_DIM, RET_HEAD_DIM), F32),
    ]
    return pl.pallas_call(
        functools.partial(_layer_kernel, final=final),
        grid=grid,
        in_specs=in_specs,
        out_specs=blk,
        out_shape=jax.ShapeDtypeStruct(x.shape, x.dtype),
        scratch_shapes=scratch,
        compiler_params=pltpu.CompilerParams(
            dimension_semantics=("arbitrary", "arbitrary"),
            vmem_limit_bytes=VMEM_LIMIT_BYTES),
        name="hybrid_layer_final" if final else "hybrid_layer",
    )(x, gain, win, wout, bias, cos, sin, dmat, qdec, kdec, bdec, fgain)


def _attention_bias(rel_bias):
    i = jnp.arange(ATT_QBLOCK)[:, None]
    j = jnp.arange(ATT_KBLOCK)[None, :]
    dist = i + HIST - j
    table = rel_bias.astype(F32)[:, jnp.clip(dist, -REL_CLIP, REL_CLIP) + REL_CLIP]
    qc = i // CHUNK
    kc = j // CHUNK
    in_band = jnp.logical_and(kc >= qc, kc <= qc + LEFT_CHUNKS)
    return jnp.where(in_band[None], table, NEG_INF)


def _retention_tables():
    h = jnp.arange(RET_HEADS, dtype=F32)
    log_g = jnp.log1p(-jnp.exp2(-5.0 - h))
    idx = jnp.arange(RET_BLOCK, dtype=F32)
    chunk = jnp.arange(RET_BLOCK) // CHUNK
    causal = chunk[None, :] <= chunk[:, None]
    dmat = jnp.exp(jnp.abs(idx[:, None] - idx[None, :])[None] * log_g[:, None, None])
    dmat = jnp.where(causal[None], dmat, 0.0)
    qdec = jnp.exp((idx + 1.0)[None, :] * log_g[:, None])
    kdec = jnp.exp((RET_BLOCK - 1 - idx)[None, :] * log_g[:, None])
    bdec = jnp.exp(RET_BLOCK * log_g)
    wide = lambda t: jnp.broadcast_to(t[:, :, None], (RET_HEADS, RET_BLOCK, LANES))
    bdec = jnp.broadcast_to(bdec[:, None, None], (RET_HEADS, 1, LANES))
    return dmat, wide(qdec), wide(kdec), bdec


def _rotary_tables(positions):
    half = RET_HEAD_DIM // 2
    inv_freq = 1.0 / (ROPE_BASE ** jnp.linspace(0.0, 1.0, half, dtype=F32))
    ang = positions.astype(F32)[:, None] * inv_freq[None, :]
    cos, sin = jnp.cos(ang), jnp.sin(ang)
    return jnp.concatenate([cos, cos], axis=-1), jnp.concatenate([-sin, sin], axis=-1)


def kernel(x, positions, norm_gain, w_in, w_out, rel_bias, final_gain):
    depth = w_in.shape[0]
    cos, sin = _rotary_tables(positions)
    dmat, qdec, kdec, bdec = _retention_tables()
    fgain = final_gain.astype(F32)[None, :]
    for layer in range(depth):
        x = _layer(
            x, norm_gain[layer].astype(F32)[None, :],
            w_in[layer].astype(BF16), w_out[layer].astype(BF16),
            _attention_bias(rel_bias[layer]), cos, sin, dmat, qdec, kdec, bdec, fgain,
            final=(layer == depth - 1))
    return x
```

```python
import functools

import jax
import jax.numpy as jnp
from jax import lax
from jax.experimental import pallas as pl
from jax.experimental.pallas import tpu as pltpu

D_MODEL = 1024
CHUNK = 64
ATT_HEADS = 8
ATT_HEAD_DIM = 64
ATT_WIDTH = ATT_HEADS * ATT_HEAD_DIM
LEFT_CHUNKS = 8
REL_CLIP = 256
RET_HEADS = 4
RET_HEAD_DIM = 128
RET_WIDTH = RET_HEADS * RET_HEAD_DIM
IN_WIDTH = 4 * ATT_WIDTH + 4 * RET_WIDTH
MIX_WIDTH = ATT_WIDTH + RET_WIDTH
EPS = 1e-6
ROPE_BASE = 10000.0
NEG_INF = -1e30

LANES = 128
SEQ_BLOCK = 512
HIST = LEFT_CHUNKS * CHUNK
ATT_QBLOCK = 128
ATT_KBLOCK = HIST + ATT_QBLOCK
ATT_LOOKAHEAD = 2
RET_BLOCK = 128
VMEM_LIMIT_BYTES = 56 * 1024 * 1024

BF16 = jnp.bfloat16
F32 = jnp.float32


def _dot(a, b):
    return jnp.dot(a, b, preferred_element_type=F32)


def _dot_nt(a, b):
    return lax.dot_general(a, b, (((1,), (1,)), ((), ())), preferred_element_type=F32)


def _dot_tn(a, b):
    return lax.dot_general(a, b, (((0,), (0,)), ((), ())), preferred_element_type=F32)


def _silu(g):
    return g * (1.0 / (1.0 + jnp.exp(-g)))


def _layer_kernel(x_ref, gain_ref, win_ref, wout_ref, bias_ref, cos_ref, sin_ref,
                  dmat_ref, qdec_ref, kdec_ref, bdec_ref, fgain_ref,
                  out_ref,
                  hn_s, q_s, kh_s, vh_s, ga_s, rq_s, rk_s, rkd_s, rv_s, gr_s, y_s, state_s,
                  *, final):
    s = pl.program_id(1)

    @pl.when(s == 0)
    def _():
        kh_s[0:HIST, :] = jnp.zeros((HIST, ATT_WIDTH), BF16)
        vh_s[0:HIST, :] = jnp.zeros((HIST, ATT_WIDTH), BF16)
        state_s[...] = jnp.zeros(state_s.shape, F32)

    gain = gain_ref[...]
    rows = 64

    def norm_body(i, c):
        r0 = pl.multiple_of(i * rows, rows)
        xb = x_ref[0, pl.ds(r0, rows), :]
        ms = jnp.mean(xb * xb, axis=-1, keepdims=True)
        hn_s[pl.ds(r0, rows), :] = (xb * lax.rsqrt(ms + EPS) * gain).astype(BF16)
        return c

    lax.fori_loop(0, SEQ_BLOCK // rows, norm_body, 0)

    hn = hn_s[...]

    def piece(c):
        return _dot(hn, win_ref[:, c * ATT_WIDTH:(c + 1) * ATT_WIDTH])

    q_s[...] = (piece(0) * (ATT_HEAD_DIM ** -0.5)).astype(BF16)
    kh_s[HIST:HIST + SEQ_BLOCK, :] = piece(1).astype(BF16)
    vh_s[HIST:HIST + SEQ_BLOCK, :] = piece(2).astype(BF16)
    ga_s[...] = _silu(piece(3))

    cos = cos_ref[...]
    sin = sin_ref[...]

    def rotary(t):
        return t * cos + pltpu.roll(t, RET_HEAD_DIM // 2, 1) * sin

    rq = piece(4)
    rk = piece(5)
    n_rb = SEQ_BLOCK // RET_BLOCK
    for h in range(RET_HEADS):
        cs = slice(h * RET_HEAD_DIM, (h + 1) * RET_HEAD_DIM)
        rq_s[:, cs] = rotary(rq[:, cs]).astype(BF16)
        kr = rotary(rk[:, cs]) * (RET_HEAD_DIM ** -0.5)
        rk_s[:, cs] = kr.astype(BF16)
        kdec = jnp.concatenate([kdec_ref[h]] * n_rb, axis=0)
        rkd_s[:, cs] = (kr * kdec).astype(BF16)
    rv_s[...] = piece(6).astype(BF16)
    gr_s[...] = _silu(piece(7))

    lane = lax.broadcasted_iota(jnp.int32, (ATT_QBLOCK, LANES), 1)
    low_half = lane < ATT_HEAD_DIM
    key_idx = lax.broadcasted_iota(jnp.int32, (1, ATT_KBLOCK), 1)

    def attn_body(qi, c):
        r0 = pl.multiple_of(qi * ATT_QBLOCK, ATT_QBLOCK)
        start_mask = jnp.where(jnp.logical_and(s == 0, key_idx + r0 < HIST), NEG_INF, 0.0)
        qsl = pl.ds(r0, ATT_QBLOCK)
        ksl = pl.ds(r0, ATT_KBLOCK)

        def scores(h):
            cs = slice((h // 2) * LANES, (h // 2 + 1) * LANES)
            keep = low_half if h % 2 == 0 else jnp.logical_not(low_half)
            qp = q_s[qsl, cs]
            qm = jnp.where(keep, qp, jnp.zeros_like(qp))
            return _dot_nt(qm, kh_s[ksl, cs]) + (bias_ref[h] + start_mask)

        pending = {h: scores(h) for h in range(ATT_LOOKAHEAD)}
        outs = {}
        for h in range(ATT_HEADS):
            if h + ATT_LOOKAHEAD < ATT_HEADS:
                pending[h + ATT_LOOKAHEAD] = scores(h + ATT_LOOKAHEAD)
            sc = pending.pop(h)
            cs = slice((h // 2) * LANES, (h // 2 + 1) * LANES)
            m = jnp.max(sc, axis=-1, keepdims=True)
            e = jnp.exp(sc - m)
            l = jnp.sum(e, axis=-1, keepdims=True)
            outs[h] = _dot(e.astype(BF16), vh_s[ksl, cs]) * (1.0 / l)
            if h % 2 == 1:
                o_pair = jnp.where(low_half, outs.pop(h - 1), outs.pop(h))
                y_s[qsl, cs] = (o_pair * ga_s[qsl, cs]).astype(BF16)
        return c

    lax.fori_loop(0, SEQ_BLOCK // ATT_QBLOCK, attn_body, 0)

    def ret_body(ri, c):
        r0 = pl.multiple_of(ri * RET_BLOCK, RET_BLOCK)
        rsl = pl.ds(r0, RET_BLOCK)
        heads = range(RET_HEADS)
        col = lambda h: slice(h * RET_HEAD_DIM, (h + 1) * RET_HEAD_DIM)
        att = [(_dot_nt(rq_s[rsl, col(h)], rk_s[rsl, col(h)]) * dmat_ref[h]).astype(BF16)
               for h in heads]
        inter = [qdec_ref[h] * _dot(rq_s[rsl, col(h)], state_s[h].astype(BF16)) for h in heads]
        for h in heads:
            state_s[h] = state_s[h] * bdec_ref[h] + _dot_tn(rkd_s[rsl, col(h)], rv_s[rsl, col(h)])
        for h in heads:
            o = _dot(att[h], rv_s[rsl, col(h)]) + inter[h]
            ms = jnp.mean(o * o, axis=-1, keepdims=True)
            o = o * lax.rsqrt(ms + EPS)
            ycs = slice(ATT_WIDTH + h * RET_HEAD_DIM, ATT_WIDTH + (h + 1) * RET_HEAD_DIM)
            y_s[rsl, ycs] = (o * gr_s[rsl, col(h)]).astype(BF16)
        return c

    lax.fori_loop(0, SEQ_BLOCK // RET_BLOCK, ret_body, 0)

    res = x_ref[0] + _dot(y_s[...], wout_ref[...])
    if final:
        ms = jnp.mean(res * res, axis=-1, keepdims=True)
        res = res * lax.rsqrt(ms + EPS) * fgain_ref[...]
    out_ref[0] = res

    kh_s[0:HIST, :] = kh_s[SEQ_BLOCK:SEQ_BLOCK + HIST, :]
    vh_s[0:HIST, :] = vh_s[SEQ_BLOCK:SEQ_BLOCK + HIST, :]


def _const_spec(shape):
    zeros = (0,) * len(shape)
    return pl.BlockSpec(shape, lambda b, s: zeros)


def _layer(x, gain, win, wout, bias, cos, sin, dmat, qdec, kdec, bdec, fgain, *, final):
    batch, seq, _ = x.shape
    grid = (batch, seq // SEQ_BLOCK)
    blk = pl.BlockSpec((1, SEQ_BLOCK, D_MODEL), lambda b, s: (b, s, 0))
    tab = pl.BlockSpec((SEQ_BLOCK, LANES), lambda b, s: (s, 0))
    in_specs = [
        blk,
        _const_spec(gain.shape), _const_spec(win.shape), _const_spec(wout.shape),
        _const_spec(bias.shape), tab, tab,
        _const_spec(dmat.shape), _const_spec(qdec.shape), _const_spec(kdec.shape),
        _const_spec(bdec.shape), _const_spec(fgain.shape),
    ]
    scratch = [
        pltpu.VMEM((SEQ_BLOCK, D_MODEL), BF16),
        pltpu.VMEM((SEQ_BLOCK, ATT_WIDTH), BF16),
        pltpu.VMEM((HIST + SEQ_BLOCK, ATT_WIDTH), BF16),
        pltpu.VMEM((HIST + SEQ_BLOCK, ATT_WIDTH), BF16),
        pltpu.VMEM((SEQ_BLOCK, ATT_WIDTH), F32),
        pltpu.VMEM((SEQ_BLOCK, RET_WIDTH), BF16),
        pltpu.VMEM((SEQ_BLOCK, RET_WIDTH), BF16),
        pltpu.VMEM((SEQ_BLOCK, RET_WIDTH), BF16),
        pltpu.VMEM((SEQ_BLOCK, RET_WIDTH), BF16),
        pltpu.VMEM((SEQ_BLOCK, RET_WIDTH), F32),
        pltpu.VMEM((SEQ_BLOCK, MIX_WIDTH), BF16),
        pltpu.VMEM((RET_HEADS, RET_HEAD_DIM, RET_HEAD_DIM), F32),
    ]
    return pl.pallas_call(
        functools.partial(_layer_kernel, final=final),
        grid=grid,
        in_specs=in_specs,
        out_specs=blk,
        out_shape=jax.ShapeDtypeStruct(x.shape, x.dtype),
        scratch_shapes=scratch,
        compiler_params=pltpu.CompilerParams(
            dimension_semantics=("arbitrary", "arbitrary"),
            vmem_limit_bytes=VMEM_LIMIT_BYTES),
        name="hybrid_layer_final" if final else "hybrid_layer",
    )(x, gain, win, wout, bias, cos, sin, dmat, qdec, kdec, bdec, fgain)


def _attention_bias(rel_bias):
    i = jnp.arange(ATT_QBLOCK)[:, None]
    j = jnp.arange(ATT_KBLOCK)[None, :]
    n = ATT_QBLOCK + ATT_KBLOCK - 1
    n_clipped = HIST + ATT_QBLOCK - 1 - REL_CLIP
    rb = rel_bias.astype(F32)
    g = jnp.concatenate(
        [jnp.broadcast_to(rb[:, 2 * REL_CLIP:], (ATT_HEADS, n_clipped)),
         rb[:, 2 * REL_CLIP - (n - n_clipped) + 1:][:, ::-1]], axis=1)
    flat = jnp.tile(jnp.pad(g, ((0, 0), (0, 1))), (1, ATT_QBLOCK))[:, :ATT_QBLOCK * n]
    table = flat.reshape(ATT_HEADS, ATT_QBLOCK, n)[:, :, ATT_QBLOCK - 1:]
    qc = i // CHUNK
    kc = j // CHUNK
    in_band = jnp.logical_and(kc >= qc, kc <= qc + LEFT_CHUNKS)
    return jnp.where(in_band[None], table, NEG_INF)


def _retention_tables():
    h = jnp.arange(RET_HEADS, dtype=F32)
    log_g = jnp.log1p(-jnp.exp2(-5.0 - h))
    idx = jnp.arange(RET_BLOCK, dtype=F32)
    chunk = jnp.arange(RET_BLOCK) // CHUNK
    causal = chunk[None, :] <= chunk[:, None]
    dmat = jnp.exp(jnp.abs(idx[:, None] - idx[None, :])[None] * log_g[:, None, None])
    dmat = jnp.where(causal[None], dmat, 0.0)
    qdec = jnp.exp((idx + 1.0)[None, :] * log_g[:, None])
    kdec = jnp.exp((RET_BLOCK - 1 - idx)[None, :] * log_g[:, None])
    bdec = jnp.exp(RET_BLOCK * log_g)
    wide = lambda t: jnp.broadcast_to(t[:, :, None], (RET_HEADS, RET_BLOCK, LANES))
    bdec = jnp.broadcast_to(bdec[:, None, None], (RET_HEADS, 1, LANES))
    return dmat, wide(qdec), wide(kdec), bdec


def _rotary_tables(positions):
    half = RET_HEAD_DIM // 2
    inv_freq = 1.0 / (ROPE_BASE ** jnp.linspace(0.0, 1.0, half, dtype=F32))
    ang = positions.astype(F32)[:, None] * inv_freq[None, :]
    cos, sin = jnp.cos(ang), jnp.sin(ang)
    return jnp.concatenate([cos, cos], axis=-1), jnp.concatenate([-sin, sin], axis=-1)


def kernel(x, positions, norm_gain, w_in, w_out, rel_bias, final_gain):
    depth = w_in.shape[0]
    cos, sin = _rotary_tables(positions)
    dmat, qdec, kdec, bdec = _retention_tables()
    fgain = final_gain.astype(F32)[None, :]
    for layer in range(depth):
        x = _layer(
            x, norm_gain[layer].astype(F32)[None, :],
            w_in[layer].astype(BF16), w_out[layer].astype(BF16),
            _attention_bias(rel_bias[layer]), cos, sin, dmat, qdec, kdec, bdec, fgain,
            final=(layer == depth - 1))
    return x
```

```python
import functools

import jax
import jax.numpy as jnp
from jax import lax
from jax.experimental import pallas as pl
from jax.experimental.pallas import tpu as pltpu

D_MODEL = 1024
CHUNK = 64
ATT_HEADS = 8
ATT_HEAD_DIM = 64
ATT_WIDTH = ATT_HEADS * ATT_HEAD_DIM
LEFT_CHUNKS = 8
REL_CLIP = 256
RET_HEADS = 4
RET_HEAD_DIM = 128
RET_WIDTH = RET_HEADS * RET_HEAD_DIM
IN_WIDTH = 4 * ATT_WIDTH + 4 * RET_WIDTH
MIX_WIDTH = ATT_WIDTH + RET_WIDTH
EPS = 1e-6
ROPE_BASE = 10000.0
NEG_INF = -1e30
LOG2E = 1.4426950408889634

LANES = 128
SEQ_BLOCK = 512
HIST = LEFT_CHUNKS * CHUNK
ATT_QBLOCK = 128
ATT_KBLOCK = HIST + ATT_QBLOCK
NORM_ROWS = 64
ATT_LOOKAHEAD = 2
RET_BLOCK = 128
VMEM_LIMIT_BYTES = 56 * 1024 * 1024

BF16 = jnp.bfloat16
F32 = jnp.float32


def _dot(a, b):
    return jnp.dot(a, b, preferred_element_type=F32)


def _dot_nt(a, b):
    return lax.dot_general(a, b, (((1,), (1,)), ((), ())), preferred_element_type=F32)


def _dot_tn(a, b):
    return lax.dot_general(a, b, (((0,), (0,)), ((), ())), preferred_element_type=F32)


def _silu(g):
    return g * (1.0 / (1.0 + jnp.exp(-g)))


def _layer_kernel(x_ref, gain_ref, win_ref, wout_ref, bias_ref, cos_ref, sin_ref,
                  dmat_ref, qdec_ref, kdec_ref, bdec_ref, fgain_ref,
                  out_ref,
                  hn_s, q_s, kh_s, vh_s, ga_s, rq_s, rqd_s, rk_s, rkd_s, rv_s, gr_s, y_s, state_s,
                  *, final):
    s = pl.program_id(1)

    @pl.when(s == 0)
    def _():
        kh_s[0:HIST, :] = jnp.zeros((HIST, ATT_WIDTH), BF16)
        vh_s[0:HIST, :] = jnp.zeros((HIST, ATT_WIDTH), BF16)
        state_s[...] = jnp.zeros(state_s.shape, F32)

    gain = gain_ref[...]
    for r0 in range(0, SEQ_BLOCK, NORM_ROWS):
        xb = x_ref[0, r0:r0 + NORM_ROWS, :]
        ms = jnp.mean(xb * xb, axis=-1, keepdims=True)
        hn_s[r0:r0 + NORM_ROWS, :] = (xb * lax.rsqrt(ms + EPS) * gain).astype(BF16)

    hn = hn_s[...]

    def piece(c):
        return _dot(hn, win_ref[:, c * ATT_WIDTH:(c + 1) * ATT_WIDTH])

    q_s[...] = (piece(0) * (ATT_HEAD_DIM ** -0.5 * LOG2E)).astype(BF16)
    kh_s[HIST:HIST + SEQ_BLOCK, :] = piece(1).astype(BF16)
    vh_s[HIST:HIST + SEQ_BLOCK, :] = piece(2).astype(BF16)
    ga_s[...] = _silu(piece(3))

    cos = cos_ref[...]
    sin = sin_ref[...]

    def rotary(t):
        return t * cos + pltpu.roll(t, RET_HEAD_DIM // 2, 1) * sin

    rq = piece(4)
    rk = piece(5)
    n_rb = SEQ_BLOCK // RET_BLOCK
    for h in range(RET_HEADS):
        cs = slice(h * RET_HEAD_DIM, (h + 1) * RET_HEAD_DIM)
        qr = rotary(rq[:, cs])
        rq_s[:, cs] = qr.astype(BF16)
        qdec = jnp.concatenate([qdec_ref[h]] * n_rb, axis=0)
        rqd_s[:, cs] = (qr * qdec).astype(BF16)
        kr = rotary(rk[:, cs]) * (RET_HEAD_DIM ** -0.5)
        rk_s[:, cs] = kr.astype(BF16)
        kdec = jnp.concatenate([kdec_ref[h]] * n_rb, axis=0)
        rkd_s[:, cs] = (kr * kdec).astype(BF16)
    rv_s[...] = piece(6).astype(BF16)
    gr_s[...] = _silu(piece(7))

    lane = lax.broadcasted_iota(jnp.int32, (ATT_QBLOCK, LANES), 1)
    low_half = lane < ATT_HEAD_DIM
    key_idx = lax.broadcasted_iota(jnp.int32, (1, ATT_KBLOCK), 1)

    def attention(first_block):
        items = [(qi, h) for qi in range(SEQ_BLOCK // ATT_QBLOCK) for h in range(ATT_HEADS)]

        def scores(qi, h):
            r0 = qi * ATT_QBLOCK
            cs = slice((h // 2) * LANES, (h // 2 + 1) * LANES)
            keep = low_half if h % 2 == 0 else jnp.logical_not(low_half)
            qp = q_s[r0:r0 + ATT_QBLOCK, cs]
            qm = jnp.where(keep, qp, jnp.zeros_like(qp))
            bias = bias_ref[h]
            if first_block:
                bias = bias + jnp.where(key_idx + r0 < HIST, NEG_INF, 0.0)
            return _dot_nt(qm, kh_s[r0:r0 + ATT_KBLOCK, cs]) + bias

        pending = {it: scores(*it) for it in items[:ATT_LOOKAHEAD]}
        outs = {}
        for n, (qi, h) in enumerate(items):
            if n + ATT_LOOKAHEAD < len(items):
                nxt = items[n + ATT_LOOKAHEAD]
                pending[nxt] = scores(*nxt)
            sc = pending.pop((qi, h))
            r0 = qi * ATT_QBLOCK
            hp = h // 2
            m = jnp.max(sc, axis=-1, keepdims=True)
            e = jnp.exp2(sc - m)
            l = jnp.sum(e, axis=-1, keepdims=True)
            cs = slice(hp * LANES, (hp + 1) * LANES)
            outs[h] = _dot(e.astype(BF16), vh_s[r0:r0 + ATT_KBLOCK, cs]) * (1.0 / l)
            if h % 2 == 1:
                o_pair = jnp.where(low_half, outs.pop(h - 1), outs.pop(h))
                y_s[r0:r0 + ATT_QBLOCK, cs] = (
                    o_pair * ga_s[r0:r0 + ATT_QBLOCK, cs]).astype(BF16)

    @pl.when(s == 0)
    def _():
        attention(first_block=True)

    @pl.when(s != 0)
    def _():
        attention(first_block=False)

    ret_items = [(ri, h) for ri in range(n_rb) for h in range(RET_HEADS)]
    rows = lambda ri: slice(ri * RET_BLOCK, (ri + 1) * RET_BLOCK)
    col = lambda h: slice(h * RET_HEAD_DIM, (h + 1) * RET_HEAD_DIM)
    kv = {(ri, h): _dot_tn(rkd_s[rows(ri), col(h)], rv_s[rows(ri), col(h)]) for ri, h in ret_items}
    att = {(ri, h): (_dot_nt(rq_s[rows(ri), col(h)], rk_s[rows(ri), col(h)]) * dmat_ref[h]).astype(BF16)
           for ri, h in ret_items}
    state_before = {}
    for h in range(RET_HEADS):
        st = state_s[h]
        for ri in range(n_rb):
            state_before[ri, h] = st.astype(BF16)
            st = st * bdec_ref[h] + kv[ri, h]
        state_s[h] = st
    for ri, h in ret_items:
        lhs = jnp.concatenate([att[ri, h], rqd_s[rows(ri), col(h)]], axis=1)
        rhs = jnp.concatenate([rv_s[rows(ri), col(h)], state_before[ri, h]], axis=0)
        o = _dot(lhs, rhs)
        ms = jnp.mean(o * o, axis=-1, keepdims=True)
        o = o * lax.rsqrt(ms + EPS)
        ycs = slice(ATT_WIDTH + h * RET_HEAD_DIM, ATT_WIDTH + (h + 1) * RET_HEAD_DIM)
        y_s[rows(ri), ycs] = (o * gr_s[rows(ri), col(h)]).astype(BF16)

    res = x_ref[0] + _dot(y_s[...], wout_ref[...])
    if final:
        ms = jnp.mean(res * res, axis=-1, keepdims=True)
        res = res * lax.rsqrt(ms + EPS) * fgain_ref[...]
    out_ref[0] = res

    kh_s[0:HIST, :] = kh_s[SEQ_BLOCK:SEQ_BLOCK + HIST, :]
    vh_s[0:HIST, :] = vh_s[SEQ_BLOCK:SEQ_BLOCK + HIST, :]


def _const_spec(shape):
    zeros = (0,) * len(shape)
    return pl.BlockSpec(shape, lambda b, s: zeros)


def _layer(x, gain, win, wout, bias, cos, sin, dmat, qdec, kdec, bdec, fgain, *, final):
    batch, seq, _ = x.shape
    grid = (batch, seq // SEQ_BLOCK)
    blk = pl.BlockSpec((1, SEQ_BLOCK, D_MODEL), lambda b, s: (b, s, 0))
    tab = pl.BlockSpec((SEQ_BLOCK, LANES), lambda b, s: (s, 0))
    in_specs = [
        blk,
        _const_spec(gain.shape), _const_spec(win.shape), _const_spec(wout.shape),
        _const_spec(bias.shape), tab, tab,
        _const_spec(dmat.shape), _const_spec(qdec.shape), _const_spec(kdec.shape),
        _const_spec(bdec.shape), _const_spec(fgain.shape),
    ]
    scratch = [
        pltpu.VMEM((SEQ_BLOCK, D_MODEL), BF16),
        pltpu.VMEM((SEQ_BLOCK, ATT_WIDTH), BF16),
        pltpu.VMEM((HIST + SEQ_BLOCK, ATT_WIDTH), BF16),
        pltpu.VMEM((HIST + SEQ_BLOCK, ATT_WIDTH), BF16),
        pltpu.VMEM((SEQ_BLOCK, ATT_WIDTH), F32),
        pltpu.VMEM((SEQ_BLOCK, RET_WIDTH), BF16),
        pltpu.VMEM((SEQ_BLOCK, RET_WIDTH), BF16),
        pltpu.VMEM((SEQ_BLOCK, RET_WIDTH), BF16),
        pltpu.VMEM((SEQ_BLOCK, RET_WIDTH), BF16),
        pltpu.VMEM((SEQ_BLOCK, RET_WIDTH), BF16),
        pltpu.VMEM((SEQ_BLOCK, RET_WIDTH), F32),
        pltpu.VMEM((SEQ_BLOCK, MIX_WIDTH), BF16),
        pltpu.VMEM((RET_HEADS, RET_HEAD_DIM, RET_HEAD_DIM), F32),
    ]
    return pl.pallas_call(
        functools.partial(_layer_kernel, final=final),
        grid=grid,
        in_specs=in_specs,
        out_specs=blk,
        out_shape=jax.ShapeDtypeStruct(x.shape, x.dtype),
        scratch_shapes=scratch,
        compiler_params=pltpu.CompilerParams(
            dimension_semantics=("arbitrary", "arbitrary"),
            vmem_limit_bytes=VMEM_LIMIT_BYTES),
        name="hybrid_layer_final" if final else "hybrid_layer",
    )(x, gain, win, wout, bias, cos, sin, dmat, qdec, kdec, bdec, fgain)


def _attention_bias(rel_bias):
    i = jnp.arange(ATT_QBLOCK)[:, None]
    j = jnp.arange(ATT_KBLOCK)[None, :]
    n = ATT_QBLOCK + ATT_KBLOCK - 1
    n_clipped = HIST + ATT_QBLOCK - 1 - REL_CLIP
    rb = rel_bias.astype(F32)
    g = jnp.concatenate(
        [jnp.broadcast_to(rb[:, 2 * REL_CLIP:], (ATT_HEADS, n_clipped)),
         rb[:, 2 * REL_CLIP - (n - n_clipped) + 1:][:, ::-1]], axis=1)
    flat = jnp.tile(jnp.pad(g, ((0, 0), (0, 1))), (1, ATT_QBLOCK))[:, :ATT_QBLOCK * n]
    table = flat.reshape(ATT_HEADS, ATT_QBLOCK, n)[:, :, ATT_QBLOCK - 1:]
    qc = i // CHUNK
    kc = j // CHUNK
    in_band = jnp.logical_and(kc >= qc, kc <= qc + LEFT_CHUNKS)
    return jnp.where(in_band[None], table * LOG2E, NEG_INF)


def _retention_tables():
    h = jnp.arange(RET_HEADS, dtype=F32)
    log_g = jnp.log1p(-jnp.exp2(-5.0 - h))
    idx = jnp.arange(RET_BLOCK, dtype=F32)
    chunk = jnp.arange(RET_BLOCK) // CHUNK
    causal = chunk[None, :] <= chunk[:, None]
    dmat = jnp.exp(jnp.abs(idx[:, None] - idx[None, :])[None] * log_g[:, None, None])
    dmat = jnp.where(causal[None], dmat, 0.0)
    qdec = jnp.exp((idx + 1.0)[None, :] * log_g[:, None])
    kdec = jnp.exp((RET_BLOCK - 1 - idx)[None, :] * log_g[:, None])
    bdec = jnp.exp(RET_BLOCK * log_g)
    wide = lambda t: jnp.broadcast_to(t[:, :, None], (RET_HEADS, RET_BLOCK, LANES))
    bdec = jnp.broadcast_to(bdec[:, None, None], (RET_HEADS, 1, LANES))
    return dmat, wide(qdec), wide(kdec), bdec


def _rotary_tables(positions):
    half = RET_HEAD_DIM // 2
    inv_freq = 1.0 / (ROPE_BASE ** jnp.linspace(0.0, 1.0, half, dtype=F32))
    ang = positions.astype(F32)[:, None] * inv_freq[None, :]
    cos, sin = jnp.cos(ang), jnp.sin(ang)
    return jnp.concatenate([cos, cos], axis=-1), jnp.concatenate([-sin, sin], axis=-1)


def kernel(x, positions, norm_gain, w_in, w_out, rel_bias, final_gain):
    depth = w_in.shape[0]
    cos, sin = _rotary_tables(positions)
    dmat, qdec, kdec, bdec = _retention_tables()
    fgain = final_gain.astype(F32)[None, :]
    for layer in range(depth):
        x = _layer(
            x, norm_gain[layer].astype(F32)[None, :],
            w_in[layer].astype(BF16), w_out[layer].astype(BF16),
            _attention_bias(rel_bias[layer]), cos, sin, dmat, qdec, kdec, bdec, fgain,
            final=(layer == depth - 1))
    return x
```

```python
import functools

import jax
import jax.numpy as jnp
from jax import lax
from jax.experimental import pallas as pl
from jax.experimental.pallas import tpu as pltpu

D_MODEL = 1024
CHUNK = 64
ATT_HEADS = 8
ATT_HEAD_DIM = 64
ATT_WIDTH = ATT_HEADS * ATT_HEAD_DIM
LEFT_CHUNKS = 8
REL_CLIP = 256
RET_HEADS = 4
RET_HEAD_DIM = 128
RET_WIDTH = RET_HEADS * RET_HEAD_DIM
IN_WIDTH = 4 * ATT_WIDTH + 4 * RET_WIDTH
MIX_WIDTH = ATT_WIDTH + RET_WIDTH
EPS = 1e-6
ROPE_BASE = 10000.0
NEG_INF = -1e30
LOG2E = 1.4426950408889634

LANES = 128
SEQ_BLOCK = 512
HIST = LEFT_CHUNKS * CHUNK
NORM_ROWS = 64
ATT_QBLOCK = 128
ATT_KBLOCK = HIST + ATT_QBLOCK
ATT_PAIRS = ATT_HEADS // 2
ATT_LOOKAHEAD = 3
RET_BLOCK = 128
VMEM_LIMIT_BYTES = 56 * 1024 * 1024

BF16 = jnp.bfloat16
F32 = jnp.float32

PIECE_AQ, PIECE_AK, PIECE_AV, PIECE_AG, PIECE_RQ, PIECE_RK, PIECE_RV, PIECE_RG = range(8)


def _dot(a, b):
    return jnp.dot(a, b, preferred_element_type=F32)


def _dot_nt(a, b):
    return lax.dot_general(a, b, (((1,), (1,)), ((), ())), preferred_element_type=F32)


def _dot_tn(a, b):
    return lax.dot_general(a, b, (((0,), (0,)), ((), ())), preferred_element_type=F32)


def _silu(g):
    return g * (1.0 / (1.0 + jnp.exp(-g)))


def _layer_kernel(x_ref, gain_ref, win_ref, wvt_ref, wout_ref, bias_ref, cos_ref, sin_ref,
                  dmat_ref, qdec_ref, kdec_ref, bdec_ref, fgain_ref,
                  out_ref,
                  hn_s, q_s, kh_s, vt_s, ga_s, rq_s, rqd_s, rk_s, rkd_s, rv_s, gr_s, y_s, state_s,
                  *, final):
    s = pl.program_id(1)

    @pl.when(s == 0)
    def _():
        kh_s[0:HIST, :] = jnp.zeros((HIST, ATT_WIDTH), BF16)
        vt_s[:, 0:HIST] = jnp.zeros((ATT_WIDTH, HIST), BF16)
        state_s[...] = jnp.zeros(state_s.shape, F32)

    gain = gain_ref[...]
    for r0 in range(0, SEQ_BLOCK, NORM_ROWS):
        xb = x_ref[0, r0:r0 + NORM_ROWS, :]
        ms = jnp.mean(xb * xb, axis=-1, keepdims=True)
        hn_s[r0:r0 + NORM_ROWS, :] = (xb * lax.rsqrt(ms + EPS) * gain).astype(BF16)

    hn = hn_s[...]

    def piece(c):
        return _dot(hn, win_ref[:, c * ATT_WIDTH:(c + 1) * ATT_WIDTH])

    q_s[...] = (piece(PIECE_AQ) * (ATT_HEAD_DIM ** -0.5 * LOG2E)).astype(BF16)
    kh_s[HIST:HIST + SEQ_BLOCK, :] = piece(PIECE_AK).astype(BF16)
    vt_s[:, HIST:HIST + SEQ_BLOCK] = _dot_nt(wvt_ref[...], hn).astype(BF16)
    ga_s[...] = _silu(piece(PIECE_AG))

    cos = cos_ref[...]
    sin = sin_ref[...]

    def rotary(t):
        return t * cos + pltpu.roll(t, RET_HEAD_DIM // 2, 1) * sin

    rq = piece(PIECE_RQ)
    rk = piece(PIECE_RK)
    n_rb = SEQ_BLOCK // RET_BLOCK
    for h in range(RET_HEADS):
        cs = slice(h * RET_HEAD_DIM, (h + 1) * RET_HEAD_DIM)
        qr = rotary(rq[:, cs])
        rq_s[:, cs] = qr.astype(BF16)
        qdec = jnp.concatenate([qdec_ref[h]] * n_rb, axis=0)
        rqd_s[:, cs] = (qr * qdec).astype(BF16)
        kr = rotary(rk[:, cs]) * (RET_HEAD_DIM ** -0.5)
        rk_s[:, cs] = kr.astype(BF16)
        kdec = jnp.concatenate([kdec_ref[h]] * n_rb, axis=0)
        rkd_s[:, cs] = (kr * kdec).astype(BF16)
    gr_s[...] = _silu(piece(PIECE_RG))
    rv_s[...] = piece(PIECE_RV).astype(BF16)

    lane = lax.broadcasted_iota(jnp.int32, (ATT_QBLOCK, LANES), 1)
    low_half = lane < ATT_HEAD_DIM
    key_idx = lax.broadcasted_iota(jnp.int32, (ATT_KBLOCK, 2 * ATT_QBLOCK), 0)

    def attention(first_block):
        items = [(qi, hp) for qi in range(SEQ_BLOCK // ATT_QBLOCK) for hp in range(ATT_PAIRS)]

        def scores_t(qi, hp):
            r0 = qi * ATT_QBLOCK
            cs = slice(hp * LANES, (hp + 1) * LANES)
            qp = q_s[r0:r0 + ATT_QBLOCK, cs]
            zero = jnp.zeros_like(qp)
            qq = jnp.concatenate([jnp.where(low_half, qp, zero), jnp.where(low_half, zero, qp)],
                                 axis=0)
            bias = bias_ref[hp]
            if first_block:
                bias = bias + jnp.where(key_idx + r0 < HIST, NEG_INF, 0.0)
            return _dot_nt(kh_s[r0:r0 + ATT_KBLOCK, cs], qq) + bias

        pending = {it: scores_t(*it) for it in items[:ATT_LOOKAHEAD]}
        for n, (qi, hp) in enumerate(items):
            if n + ATT_LOOKAHEAD < len(items):
                nxt = items[n + ATT_LOOKAHEAD]
                pending[nxt] = scores_t(*nxt)
            sc = pending.pop((qi, hp))
            r0 = qi * ATT_QBLOCK
            cs = slice(hp * LANES, (hp + 1) * LANES)
            m = jnp.max(sc, axis=0, keepdims=True)
            e = jnp.exp2(sc - m)
            l = jnp.sum(e, axis=0, keepdims=True)
            ot = _dot(vt_s[cs, r0:r0 + ATT_KBLOCK], e.astype(BF16)) * (1.0 / l)
            o_pair_t = jnp.concatenate([ot[:ATT_HEAD_DIM, :ATT_QBLOCK],
                                        ot[ATT_HEAD_DIM:, ATT_QBLOCK:]], axis=0)
            y_s[r0:r0 + ATT_QBLOCK, cs] = (
                o_pair_t.T * ga_s[r0:r0 + ATT_QBLOCK, cs]).astype(BF16)

    @pl.when(s == 0)
    def _():
        attention(first_block=True)

    @pl.when(s != 0)
    def _():
        attention(first_block=False)

    ret_items = [(ri, h) for ri in range(n_rb) for h in range(RET_HEADS)]
    rows = lambda ri: slice(ri * RET_BLOCK, (ri + 1) * RET_BLOCK)
    col = lambda h: slice(h * RET_HEAD_DIM, (h + 1) * RET_HEAD_DIM)
    kv = {(ri, h): _dot_tn(rkd_s[rows(ri), col(h)], rv_s[rows(ri), col(h)]) for ri, h in ret_items}
    att = {(ri, h): (_dot_nt(rq_s[rows(ri), col(h)], rk_s[rows(ri), col(h)]) * dmat_ref[h]).astype(BF16)
           for ri, h in ret_items}
    state_before = {}
    for h in range(RET_HEADS):
        st = state_s[h]
        for ri in range(n_rb):
            state_before[ri, h] = st.astype(BF16)
            st = st * bdec_ref[h] + kv[ri, h]
        state_s[h] = st
    for ri, h in ret_items:
        lhs = jnp.concatenate([att[ri, h], rqd_s[rows(ri), col(h)]], axis=1)
        rhs = jnp.concatenate([rv_s[rows(ri), col(h)], state_before[ri, h]], axis=0)
        o = _dot(lhs, rhs)
        ms = jnp.mean(o * o, axis=-1, keepdims=True)
        o = o * lax.rsqrt(ms + EPS)
        ycs = slice(ATT_WIDTH + h * RET_HEAD_DIM, ATT_WIDTH + (h + 1) * RET_HEAD_DIM)
        y_s[rows(ri), ycs] = (o * gr_s[rows(ri), col(h)]).astype(BF16)

    res = x_ref[0] + _dot(y_s[...], wout_ref[...])
    if final:
        ms = jnp.mean(res * res, axis=-1, keepdims=True)
        res = res * lax.rsqrt(ms + EPS) * fgain_ref[...]
    out_ref[0] = res

    kh_s[0:HIST, :] = kh_s[SEQ_BLOCK:SEQ_BLOCK + HIST, :]
    vt_s[:, 0:HIST] = vt_s[:, SEQ_BLOCK:SEQ_BLOCK + HIST]


def _const_spec(shape):
    zeros = (0,) * len(shape)
    return pl.BlockSpec(shape, lambda b, s: zeros)


def _layer(x, gain, win, wvt, wout, bias, cos, sin, dmat, qdec, kdec, bdec, fgain, *, final):
    batch, seq, _ = x.shape
    grid = (batch, seq // SEQ_BLOCK)
    blk = pl.BlockSpec((1, SEQ_BLOCK, D_MODEL), lambda b, s: (b, s, 0))
    tab = pl.BlockSpec((SEQ_BLOCK, LANES), lambda b, s: (s, 0))
    in_specs = [
        blk,
        _const_spec(gain.shape), _const_spec(win.shape), _const_spec(wvt.shape),
        _const_spec(wout.shape), _const_spec(bias.shape), tab, tab,
        _const_spec(dmat.shape), _const_spec(qdec.shape), _const_spec(kdec.shape),
        _const_spec(bdec.shape), _const_spec(fgain.shape),
    ]
    scratch = [
        pltpu.VMEM((SEQ_BLOCK, D_MODEL), BF16),
        pltpu.VMEM((SEQ_BLOCK, ATT_WIDTH), BF16),
        pltpu.VMEM((HIST + SEQ_BLOCK, ATT_WIDTH), BF16),
        pltpu.VMEM((ATT_WIDTH, HIST + SEQ_BLOCK), BF16),
        pltpu.VMEM((SEQ_BLOCK, ATT_WIDTH), F32),
        pltpu.VMEM((SEQ_BLOCK, RET_WIDTH), BF16),
        pltpu.VMEM((SEQ_BLOCK, RET_WIDTH), BF16),
        pltpu.VMEM((SEQ_BLOCK, RET_WIDTH), BF16),
        pltpu.VMEM((SEQ_BLOCK, RET_WIDTH), BF16),
        pltpu.VMEM((SEQ_BLOCK, RET_WIDTH), BF16),
        pltpu.VMEM((SEQ_BLOCK, RET_WIDTH), F32),
        pltpu.VMEM((SEQ_BLOCK, MIX_WIDTH), BF16),
        pltpu.VMEM((RET_HEADS, RET_HEAD_DIM, RET_HEAD_DIM), F32),
    ]
    return pl.pallas_call(
        functools.partial(_layer_kernel, final=final),
        grid=grid,
        in_specs=in_specs,
        out_specs=blk,
        out_shape=jax.ShapeDtypeStruct(x.shape, x.dtype),
        scratch_shapes=scratch,
        compiler_params=pltpu.CompilerParams(
            dimension_semantics=("arbitrary", "arbitrary"),
            vmem_limit_bytes=VMEM_LIMIT_BYTES),
        name="hybrid_layer_final" if final else "hybrid_layer",
    )(x, gain, win, wvt, wout, bias, cos, sin, dmat, qdec, kdec, bdec, fgain)


def _attention_bias_t(rel_bias):
    i = jnp.arange(ATT_QBLOCK)[:, None]
    j = jnp.arange(ATT_KBLOCK)[None, :]
    n = ATT_QBLOCK + ATT_KBLOCK - 1
    n_clipped = HIST + ATT_QBLOCK - 1 - REL_CLIP
    rb = rel_bias.astype(F32)
    g = jnp.concatenate(
        [jnp.broadcast_to(rb[:, 2 * REL_CLIP:], (ATT_HEADS, n_clipped)),
         rb[:, 2 * REL_CLIP - (n - n_clipped) + 1:][:, ::-1]], axis=1)
    flat = jnp.tile(jnp.pad(g, ((0, 0), (0, 1))), (1, ATT_QBLOCK))[:, :ATT_QBLOCK * n]
    table = flat.reshape(ATT_HEADS, ATT_QBLOCK, n)[:, :, ATT_QBLOCK - 1:]
    qc = i // CHUNK
    kc = j // CHUNK
    in_band = jnp.logical_and(kc >= qc, kc <= qc + LEFT_CHUNKS)
    bias = jnp.where(in_band[None], table * LOG2E, NEG_INF)
    bias_t = jnp.swapaxes(bias, 1, 2).reshape(ATT_PAIRS, 2, ATT_KBLOCK, ATT_QBLOCK)
    return jnp.swapaxes(bias_t, 1, 2).reshape(ATT_PAIRS, ATT_KBLOCK, 2 * ATT_QBLOCK)


def _retention_tables():
    h = jnp.arange(RET_HEADS, dtype=F32)
    log_g = jnp.log1p(-jnp.exp2(-5.0 - h))
    idx = jnp.arange(RET_BLOCK, dtype=F32)
    chunk = jnp.arange(RET_BLOCK) // CHUNK
    causal = chunk[None, :] <= chunk[:, None]
    dmat = jnp.exp(jnp.abs(idx[:, None] - idx[None, :])[None] * log_g[:, None, None])
    dmat = jnp.where(causal[None], dmat, 0.0)
    qdec = jnp.exp((idx + 1.0)[None, :] * log_g[:, None])
    kdec = jnp.exp((RET_BLOCK - 1 - idx)[None, :] * log_g[:, None])
    bdec = jnp.exp(RET_BLOCK * log_g)
    wide = lambda t: jnp.broadcast_to(t[:, :, None], (RET_HEADS, RET_BLOCK, LANES))
    bdec = jnp.broadcast_to(bdec[:, None, None], (RET_HEADS, 1, LANES))
    return dmat, wide(qdec), wide(kdec), bdec


def _rotary_tables(positions):
    half = RET_HEAD_DIM // 2
    inv_freq = 1.0 / (ROPE_BASE ** jnp.linspace(0.0, 1.0, half, dtype=F32))
    ang = positions.astype(F32)[:, None] * inv_freq[None, :]
    cos, sin = jnp.cos(ang), jnp.sin(ang)
    return jnp.concatenate([cos, cos], axis=-1), jnp.concatenate([-sin, sin], axis=-1)


def kernel(x, positions, norm_gain, w_in, w_out, rel_bias, final_gain):
    depth = w_in.shape[0]
    cos, sin = _rotary_tables(positions)
    dmat, qdec, kdec, bdec = _retention_tables()
    fgain = final_gain.astype(F32)[None, :]
    for layer in range(depth):
        win = w_in[layer].astype(BF16)
        wvt = win[:, PIECE_AV * ATT_WIDTH:(PIECE_AV + 1) * ATT_WIDTH].T
        x = _layer(
            x, norm_gain[layer].astype(F32)[None, :], win, wvt, w_out[layer].astype(BF16),
            _attention_bias_t(rel_bias[layer]), cos, sin, dmat, qdec, kdec, bdec, fgain,
            final=(layer == depth - 1))
    return x
```

```python
import functools

import jax
import jax.numpy as jnp
from jax import lax
from jax.experimental import pallas as pl
from jax.experimental.pallas import tpu as pltpu

D_MODEL = 1024
CHUNK = 64
ATT_HEADS = 8
ATT_HEAD_DIM = 64
ATT_WIDTH = ATT_HEADS * ATT_HEAD_DIM
LEFT_CHUNKS = 8
REL_CLIP = 256
RET_HEADS = 4
RET_HEAD_DIM = 128
RET_WIDTH = RET_HEADS * RET_HEAD_DIM
IN_WIDTH = 4 * ATT_WIDTH + 4 * RET_WIDTH
MIX_WIDTH = ATT_WIDTH + RET_WIDTH
EPS = 1e-6
ROPE_BASE = 10000.0
NEG_INF = -1e30
LOG2E = 1.4426950408889634

LANES = 128
SEQ_BLOCK = 512
HIST = LEFT_CHUNKS * CHUNK
NORM_ROWS = 64
ATT_QBLOCK = 128
ATT_KBLOCK = HIST + ATT_QBLOCK
ATT_PAIRS = ATT_HEADS // 2
ATT_LOOKAHEAD = 3
RET_BLOCK = 128
VMEM_LIMIT_BYTES = 56 * 1024 * 1024

BF16 = jnp.bfloat16
F32 = jnp.float32

PIECE_AQ, PIECE_AK, PIECE_AV, PIECE_AG, PIECE_RQ, PIECE_RK, PIECE_RV, PIECE_RG = range(8)


def _dot(a, b):
    return jnp.dot(a, b, preferred_element_type=F32)


def _dot_nt(a, b):
    return lax.dot_general(a, b, (((1,), (1,)), ((), ())), preferred_element_type=F32)


def _dot_tn(a, b):
    return lax.dot_general(a, b, (((0,), (0,)), ((), ())), preferred_element_type=F32)


def _silu(g):
    return g * (1.0 / (1.0 + jnp.exp(-g)))


def _layer_kernel(x_ref, gain_ref, win_ref, wvt_ref, wout_ref, bias_ref, cos_ref, sin_ref,
                  dmat_ref, qdec_ref, kdec_ref, bdec_ref, fgain_ref,
                  out_ref,
                  hn_s, q_s, kh_s, vt_s, ga_s, rq_s, rqd_s, rk_s, rkd_s, rv_s, gr_s, y_s, state_s,
                  *, final):
    s = pl.program_id(1)

    @pl.when(s == 0)
    def _():
        kh_s[0:HIST, :] = jnp.zeros((HIST, ATT_WIDTH), BF16)
        vt_s[:, 0:HIST] = jnp.zeros((ATT_WIDTH, HIST), BF16)
        state_s[...] = jnp.zeros(state_s.shape, F32)

    gain = gain_ref[...]
    for r0 in range(0, SEQ_BLOCK, NORM_ROWS):
        xb = x_ref[0, r0:r0 + NORM_ROWS, :]
        ms = jnp.mean(xb * xb, axis=-1, keepdims=True)
        hn_s[r0:r0 + NORM_ROWS, :] = (xb * lax.rsqrt(ms + EPS) * gain).astype(BF16)

    hn = hn_s[...]

    def piece(c):
        return _dot(hn, win_ref[:, c * ATT_WIDTH:(c + 1) * ATT_WIDTH])

    q_s[...] = (piece(PIECE_AQ) * (ATT_HEAD_DIM ** -0.5 * LOG2E)).astype(BF16)
    kh_s[HIST:HIST + SEQ_BLOCK, :] = piece(PIECE_AK).astype(BF16)
    vt_s[:, HIST:HIST + SEQ_BLOCK] = _dot_nt(wvt_ref[...], hn).astype(BF16)
    ga_s[...] = _silu(piece(PIECE_AG))

    cos = cos_ref[...]
    sin = sin_ref[...]

    def rotary(t):
        return t * cos + pltpu.roll(t, RET_HEAD_DIM // 2, 1) * sin

    rq = piece(PIECE_RQ)
    rk = piece(PIECE_RK)
    n_rb = SEQ_BLOCK // RET_BLOCK
    for h in range(RET_HEADS):
        cs = slice(h * RET_HEAD_DIM, (h + 1) * RET_HEAD_DIM)
        qr = rotary(rq[:, cs])
        rq_s[:, cs] = qr.astype(BF16)
        qdec = jnp.concatenate([qdec_ref[h]] * n_rb, axis=0)
        rqd_s[:, cs] = (qr * qdec).astype(BF16)
        kr = rotary(rk[:, cs]) * (RET_HEAD_DIM ** -0.5)
        rk_s[:, cs] = kr.astype(BF16)
        kdec = jnp.concatenate([kdec_ref[h]] * n_rb, axis=0)
        rkd_s[:, cs] = (kr * kdec).astype(BF16)
    gr_s[...] = _silu(piece(PIECE_RG))
    rv_s[...] = piece(PIECE_RV).astype(BF16)

    lane = lax.broadcasted_iota(jnp.int32, (ATT_QBLOCK, LANES), 1)
    low_half = lane < ATT_HEAD_DIM
    key_idx = lax.broadcasted_iota(jnp.int32, (ATT_KBLOCK, 2 * ATT_QBLOCK), 0)

    def attention(first_block):
        items = [(qi, hp) for qi in range(SEQ_BLOCK // ATT_QBLOCK) for hp in range(ATT_PAIRS)]

        def scores_t(qi, hp):
            r0 = qi * ATT_QBLOCK
            cs = slice(hp * LANES, (hp + 1) * LANES)
            qp = q_s[r0:r0 + ATT_QBLOCK, cs]
            zero = jnp.zeros_like(qp)
            qq = jnp.concatenate([jnp.where(low_half, qp, zero), jnp.where(low_half, zero, qp)],
                                 axis=0)
            bias = jnp.concatenate([bias_ref[2 * hp], bias_ref[2 * hp + 1]], axis=1)
            if first_block:
                bias = bias + jnp.where(key_idx + r0 < HIST, NEG_INF, 0.0)
            return _dot_nt(kh_s[r0:r0 + ATT_KBLOCK, cs], qq) + bias

        pending = {it: scores_t(*it) for it in items[:ATT_LOOKAHEAD]}
        for n, (qi, hp) in enumerate(items):
            if n + ATT_LOOKAHEAD < len(items):
                nxt = items[n + ATT_LOOKAHEAD]
                pending[nxt] = scores_t(*nxt)
            sc = pending.pop((qi, hp))
            r0 = qi * ATT_QBLOCK
            cs = slice(hp * LANES, (hp + 1) * LANES)
            m = jnp.max(sc, axis=0, keepdims=True)
            e = jnp.exp2(sc - m)
            l = jnp.sum(e, axis=0, keepdims=True)
            ot = _dot(vt_s[cs, r0:r0 + ATT_KBLOCK], e.astype(BF16)) * (1.0 / l)
            o_pair_t = jnp.concatenate([ot[:ATT_HEAD_DIM, :ATT_QBLOCK],
                                        ot[ATT_HEAD_DIM:, ATT_QBLOCK:]], axis=0)
            y_s[r0:r0 + ATT_QBLOCK, cs] = (
                o_pair_t.T * ga_s[r0:r0 + ATT_QBLOCK, cs]).astype(BF16)

    @pl.when(s == 0)
    def _():
        attention(first_block=True)

    @pl.when(s != 0)
    def _():
        attention(first_block=False)

    ret_items = [(ri, h) for ri in range(n_rb) for h in range(RET_HEADS)]
    rows = lambda ri: slice(ri * RET_BLOCK, (ri + 1) * RET_BLOCK)
    col = lambda h: slice(h * RET_HEAD_DIM, (h + 1) * RET_HEAD_DIM)
    kv = {(ri, h): _dot_tn(rkd_s[rows(ri), col(h)], rv_s[rows(ri), col(h)]) for ri, h in ret_items}
    att = {(ri, h): (_dot_nt(rq_s[rows(ri), col(h)], rk_s[rows(ri), col(h)]) * dmat_ref[h]).astype(BF16)
           for ri, h in ret_items}
    state_before = {}
    for h in range(RET_HEADS):
        st = state_s[h]
        for ri in range(n_rb):
            state_before[ri, h] = st.astype(BF16)
            st = st * bdec_ref[h] + kv[ri, h]
        state_s[h] = st
    for ri, h in ret_items:
        lhs = jnp.concatenate([att[ri, h], rqd_s[rows(ri), col(h)]], axis=1)
        rhs = jnp.concatenate([rv_s[rows(ri), col(h)], state_before[ri, h]], axis=0)
        o = _dot(lhs, rhs)
        ms = jnp.mean(o * o, axis=-1, keepdims=True)
        o = o * lax.rsqrt(ms + EPS)
        ycs = slice(ATT_WIDTH + h * RET_HEAD_DIM, ATT_WIDTH + (h + 1) * RET_HEAD_DIM)
        y_s[rows(ri), ycs] = (o * gr_s[rows(ri), col(h)]).astype(BF16)

    res = x_ref[0] + _dot(y_s[...], wout_ref[...])
    if final:
        ms = jnp.mean(res * res, axis=-1, keepdims=True)
        res = res * lax.rsqrt(ms + EPS) * fgain_ref[...]
    out_ref[0] = res

    kh_s[0:HIST, :] = kh_s[SEQ_BLOCK:SEQ_BLOCK + HIST, :]
    vt_s[:, 0:HIST] = vt_s[:, SEQ_BLOCK:SEQ_BLOCK + HIST]


def _const_spec(shape):
    zeros = (0,) * len(shape)
    return pl.BlockSpec(shape, lambda b, s: zeros)


def _layer(x, gain, win, wvt, wout, bias, cos, sin, dmat, qdec, kdec, bdec, fgain, *, final):
    batch, seq, _ = x.shape
    grid = (batch, seq // SEQ_BLOCK)
    blk = pl.BlockSpec((1, SEQ_BLOCK, D_MODEL), lambda b, s: (b, s, 0))
    tab = pl.BlockSpec((SEQ_BLOCK, LANES), lambda b, s: (s, 0))
    in_specs = [
        blk,
        _const_spec(gain.shape), _const_spec(win.shape), _const_spec(wvt.shape),
        _const_spec(wout.shape), _const_spec(bias.shape), tab, tab,
        _const_spec(dmat.shape), _const_spec(qdec.shape), _const_spec(kdec.shape),
        _const_spec(bdec.shape), _const_spec(fgain.shape),
    ]
    scratch = [
        pltpu.VMEM((SEQ_BLOCK, D_MODEL), BF16),
        pltpu.VMEM((SEQ_BLOCK, ATT_WIDTH), BF16),
        pltpu.VMEM((HIST + SEQ_BLOCK, ATT_WIDTH), BF16),
        pltpu.VMEM((ATT_WIDTH, HIST + SEQ_BLOCK), BF16),
        pltpu.VMEM((SEQ_BLOCK, ATT_WIDTH), F32),
        pltpu.VMEM((SEQ_BLOCK, RET_WIDTH), BF16),
        pltpu.VMEM((SEQ_BLOCK, RET_WIDTH), BF16),
        pltpu.VMEM((SEQ_BLOCK, RET_WIDTH), BF16),
        pltpu.VMEM((SEQ_BLOCK, RET_WIDTH), BF16),
        pltpu.VMEM((SEQ_BLOCK, RET_WIDTH), BF16),
        pltpu.VMEM((SEQ_BLOCK, RET_WIDTH), F32),
        pltpu.VMEM((SEQ_BLOCK, MIX_WIDTH), BF16),
        pltpu.VMEM((RET_HEADS, RET_HEAD_DIM, RET_HEAD_DIM), F32),
    ]
    return pl.pallas_call(
        functools.partial(_layer_kernel, final=final),
        grid=grid,
        in_specs=in_specs,
        out_specs=blk,
        out_shape=jax.ShapeDtypeStruct(x.shape, x.dtype),
        scratch_shapes=scratch,
        compiler_params=pltpu.CompilerParams(
            dimension_semantics=("arbitrary", "arbitrary"),
            vmem_limit_bytes=VMEM_LIMIT_BYTES),
        name="hybrid_layer_final" if final else "hybrid_layer",
    )(x, gain, win, wvt, wout, bias, cos, sin, dmat, qdec, kdec, bdec, fgain)


def _attention_bias_t(rel_bias):
    j = jnp.arange(ATT_KBLOCK)[:, None]
    i = jnp.arange(ATT_QBLOCK)[None, :]
    n = ATT_QBLOCK + ATT_KBLOCK - 1
    n_clipped = HIST + ATT_QBLOCK - 1 - REL_CLIP
    rb = rel_bias.astype(F32)
    g = jnp.concatenate(
        [rb[:, 2 * REL_CLIP - (n - n_clipped) + 1:],
         jnp.broadcast_to(rb[:, 2 * REL_CLIP:], (ATT_HEADS, n_clipped))], axis=1)
    flat = jnp.tile(jnp.pad(g, ((0, 0), (0, 1))), (1, ATT_KBLOCK))[:, :ATT_KBLOCK * n]
    table = flat.reshape(ATT_HEADS, ATT_KBLOCK, n)[:, :, ATT_KBLOCK - 1:]
    qc = i // CHUNK
    kc = j // CHUNK
    in_band = jnp.logical_and(kc >= qc, kc <= qc + LEFT_CHUNKS)
    return jnp.where(in_band[None], table * LOG2E, NEG_INF)


def _retention_tables():
    h = jnp.arange(RET_HEADS, dtype=F32)
    log_g = jnp.log1p(-jnp.exp2(-5.0 - h))
    idx = jnp.arange(RET_BLOCK, dtype=F32)
    chunk = jnp.arange(RET_BLOCK) // CHUNK
    causal = chunk[None, :] <= chunk[:, None]
    dmat = jnp.exp(jnp.abs(idx[:, None] - idx[None, :])[None] * log_g[:, None, None])
    dmat = jnp.where(causal[None], dmat, 0.0)
    qdec = jnp.exp((idx + 1.0)[None, :] * log_g[:, None])
    kdec = jnp.exp((RET_BLOCK - 1 - idx)[None, :] * log_g[:, None])
    bdec = jnp.exp(RET_BLOCK * log_g)
    wide = lambda t: jnp.broadcast_to(t[:, :, None], (RET_HEADS, RET_BLOCK, LANES))
    bdec = jnp.broadcast_to(bdec[:, None, None], (RET_HEADS, 1, LANES))
    return dmat, wide(qdec), wide(kdec), bdec


def _rotary_tables(positions):
    half = RET_HEAD_DIM // 2
    inv_freq = 1.0 / (ROPE_BASE ** jnp.linspace(0.0, 1.0, half, dtype=F32))
    ang = positions.astype(F32)[:, None] * inv_freq[None, :]
    cos, sin = jnp.cos(ang), jnp.sin(ang)
    return jnp.concatenate([cos, cos], axis=-1), jnp.concatenate([-sin, sin], axis=-1)


def kernel(x, positions, norm_gain, w_in, w_out, rel_bias, final_gain):
    depth = w_in.shape[0]
    cos, sin = _rotary_tables(positions)
    dmat, qdec, kdec, bdec = _retention_tables()
    fgain = final_gain.astype(F32)[None, :]
    win = w_in.astype(BF16)
    wout = w_out.astype(BF16)
    wvt = jnp.swapaxes(
        w_in[:, :, PIECE_AV * ATT_WIDTH:(PIECE_AV + 1) * ATT_WIDTH], 1, 2).astype(BF16)
    for layer in range(depth):
        x = _layer(
            x, norm_gain[layer].astype(F32)[None, :], win[layer], wvt[layer], wout[layer],
            _attention_bias_t(rel_bias[layer]), cos, sin, dmat, qdec, kdec, bdec, fgain,
            final=(layer == depth - 1))
    return x
```

```python
import functools

import jax
import jax.numpy as jnp
from jax import lax
from jax.experimental import pallas as pl
from jax.experimental.pallas import tpu as pltpu

D_MODEL = 1024
CHUNK = 64
ATT_HEADS = 8
ATT_HEAD_DIM = 64
ATT_WIDTH = ATT_HEADS * ATT_HEAD_DIM
LEFT_CHUNKS = 8
REL_CLIP = 256
RET_HEADS = 4
RET_HEAD_DIM = 128
RET_WIDTH = RET_HEADS * RET_HEAD_DIM
IN_WIDTH = 4 * ATT_WIDTH + 4 * RET_WIDTH
MIX_WIDTH = ATT_WIDTH + RET_WIDTH
EPS = 1e-6
ROPE_BASE = 10000.0
NEG_INF = -1e30
LOG2E = 1.4426950408889634

LANES = 128
SEQ_BLOCK = 512
HIST = LEFT_CHUNKS * CHUNK
NORM_ROWS = 64
ATT_QBLOCK = 128
ATT_KBLOCK = HIST + ATT_QBLOCK
ATT_PAIRS = ATT_HEADS // 2
ATT_LOOKAHEAD = 3
RET_BLOCK = 128
VMEM_LIMIT_BYTES = 56 * 1024 * 1024

BF16 = jnp.bfloat16
F32 = jnp.float32

PIECE_AQ, PIECE_AK, PIECE_AV, PIECE_AG, PIECE_RQ, PIECE_RK, PIECE_RV, PIECE_RG = range(8)


def _dot(a, b):
    return jnp.dot(a, b, preferred_element_type=F32)


def _dot_nt(a, b):
    return lax.dot_general(a, b, (((1,), (1,)), ((), ())), preferred_element_type=F32)


def _dot_tn(a, b):
    return lax.dot_general(a, b, (((0,), (0,)), ((), ())), preferred_element_type=F32)


def _silu(g):
    return g * (1.0 / (1.0 + jnp.exp(-g)))


def _layer_kernel(x_ref, gain_ref, win_ref, wout_ref, g_ref, cos_ref, sin_ref,
                  dmat_ref, qdec_ref, kdec_ref, bdec_ref, fgain_ref,
                  out_ref,
                  bias_s, hn_s, q_s, kh_s, vt_s, ga_s, rq_s, rqd_s, rk_s, rkd_s, rv_s, gr_s, y_s, state_s,
                  *, final):
    s = pl.program_id(1)

    @pl.when(jnp.logical_and(pl.program_id(0) == 0, s == 0))
    def _():
        row = lax.broadcasted_iota(jnp.int32, (ATT_QBLOCK, ATT_QBLOCK), 0)
        q_chunk = lax.broadcasted_iota(jnp.int32, (ATT_QBLOCK, ATT_QBLOCK), 1) // CHUNK
        for jb in range(0, ATT_KBLOCK, ATT_QBLOCK):
            k_chunk = (row + jb) // CHUNK
            in_band = jnp.logical_and(k_chunk >= q_chunk, k_chunk <= q_chunk + LEFT_CHUNKS)
            start = HIST - jb
            for h in range(ATT_HEADS):
                window = jnp.broadcast_to(g_ref[h:h + 1, start:start + 2 * ATT_QBLOCK],
                                          (ATT_QBLOCK, 2 * ATT_QBLOCK))
                rolled = pltpu.roll(window, ATT_QBLOCK + 1, 1, stride=1, stride_axis=0)
                bias_s[h, jb:jb + ATT_QBLOCK, :] = jnp.where(
                    in_band, rolled[:, :ATT_QBLOCK], NEG_INF)

    @pl.when(s == 0)
    def _():
        kh_s[0:HIST, :] = jnp.zeros((HIST, ATT_WIDTH), BF16)
        vt_s[:, 0:HIST] = jnp.zeros((ATT_WIDTH, HIST), BF16)
        state_s[...] = jnp.zeros(state_s.shape, F32)

    gain = gain_ref[...]
    for r0 in range(0, SEQ_BLOCK, NORM_ROWS):
        xb = x_ref[0, r0:r0 + NORM_ROWS, :]
        ms = jnp.mean(xb * xb, axis=-1, keepdims=True)
        hn_s[r0:r0 + NORM_ROWS, :] = (xb * lax.rsqrt(ms + EPS) * gain).astype(BF16)

    hn = hn_s[...]

    def piece(c):
        return _dot(hn, win_ref[:, c * ATT_WIDTH:(c + 1) * ATT_WIDTH])

    q_s[...] = (piece(PIECE_AQ) * (ATT_HEAD_DIM ** -0.5 * LOG2E)).astype(BF16)
    kh_s[HIST:HIST + SEQ_BLOCK, :] = piece(PIECE_AK).astype(BF16)
    vt_s[:, HIST:HIST + SEQ_BLOCK] = piece(PIECE_AV).T.astype(BF16)
    ga_s[...] = _silu(piece(PIECE_AG))

    cos = cos_ref[...]
    sin = sin_ref[...]

    def rotary(t):
        return t * cos + pltpu.roll(t, RET_HEAD_DIM // 2, 1) * sin

    rq = piece(PIECE_RQ)
    rk = piece(PIECE_RK)
    n_rb = SEQ_BLOCK // RET_BLOCK
    for h in range(RET_HEADS):
        cs = slice(h * RET_HEAD_DIM, (h + 1) * RET_HEAD_DIM)
        qr = rotary(rq[:, cs])
        rq_s[:, cs] = qr.astype(BF16)
        qdec = jnp.concatenate([qdec_ref[h]] * n_rb, axis=0)
        rqd_s[:, cs] = (qr * qdec).astype(BF16)
        kr = rotary(rk[:, cs]) * (RET_HEAD_DIM ** -0.5)
        rk_s[:, cs] = kr.astype(BF16)
        kdec = jnp.concatenate([kdec_ref[h]] * n_rb, axis=0)
        rkd_s[:, cs] = (kr * kdec).astype(BF16)
    gr_s[...] = _silu(piece(PIECE_RG))
    rv_s[...] = piece(PIECE_RV).astype(BF16)

    lane = lax.broadcasted_iota(jnp.int32, (ATT_QBLOCK, LANES), 1)
    low_half = lane < ATT_HEAD_DIM
    key_idx = lax.broadcasted_iota(jnp.int32, (ATT_KBLOCK, 2 * ATT_QBLOCK), 0)

    def attention(first_block):
        items = [(qi, hp) for qi in range(SEQ_BLOCK // ATT_QBLOCK) for hp in range(ATT_PAIRS)]

        def scores_t(qi, hp):
            r0 = qi * ATT_QBLOCK
            cs = slice(hp * LANES, (hp + 1) * LANES)
            qp = q_s[r0:r0 + ATT_QBLOCK, cs]
            zero = jnp.zeros_like(qp)
            qq = jnp.concatenate([jnp.where(low_half, qp, zero), jnp.where(low_half, zero, qp)],
                                 axis=0)
            bias = jnp.concatenate([bias_s[2 * hp], bias_s[2 * hp + 1]], axis=1)
            if first_block:
                bias = bias + jnp.where(key_idx + r0 < HIST, NEG_INF, 0.0)
            return _dot_nt(kh_s[r0:r0 + ATT_KBLOCK, cs], qq) + bias

        pending = {it: scores_t(*it) for it in items[:ATT_LOOKAHEAD]}
        for n, (qi, hp) in enumerate(items):
            if n + ATT_LOOKAHEAD < len(items):
                nxt = items[n + ATT_LOOKAHEAD]
                pending[nxt] = scores_t(*nxt)
            sc = pending.pop((qi, hp))
            r0 = qi * ATT_QBLOCK
            cs = slice(hp * LANES, (hp + 1) * LANES)
            m = jnp.max(sc, axis=0, keepdims=True)
            e = jnp.exp2(sc - m)
            l = jnp.sum(e, axis=0, keepdims=True)
            ot = _dot(vt_s[cs, r0:r0 + ATT_KBLOCK], e.astype(BF16)) * (1.0 / l)
            o_pair_t = jnp.concatenate([ot[:ATT_HEAD_DIM, :ATT_QBLOCK],
                                        ot[ATT_HEAD_DIM:, ATT_QBLOCK:]], axis=0)
            y_s[r0:r0 + ATT_QBLOCK, cs] = (
                o_pair_t.T * ga_s[r0:r0 + ATT_QBLOCK, cs]).astype(BF16)

    @pl.when(s == 0)
    def _():
        attention(first_block=True)

    @pl.when(s != 0)
    def _():
        attention(first_block=False)

    ret_items = [(ri, h) for ri in range(n_rb) for h in range(RET_HEADS)]
    rows = lambda ri: slice(ri * RET_BLOCK, (ri + 1) * RET_BLOCK)
    col = lambda h: slice(h * RET_HEAD_DIM, (h + 1) * RET_HEAD_DIM)
    kv = {(ri, h): _dot_tn(rkd_s[rows(ri), col(h)], rv_s[rows(ri), col(h)]) for ri, h in ret_items}
    att = {(ri, h): (_dot_nt(rq_s[rows(ri), col(h)], rk_s[rows(ri), col(h)]) * dmat_ref[h]).astype(BF16)
           for ri, h in ret_items}
    state_before = {}
    for h in range(RET_HEADS):
        st = state_s[h]
        for ri in range(n_rb):
            state_before[ri, h] = st.astype(BF16)
            st = st * bdec_ref[h] + kv[ri, h]
        state_s[h] = st
    for ri, h in ret_items:
        lhs = jnp.concatenate([att[ri, h], rqd_s[rows(ri), col(h)]], axis=1)
        rhs = jnp.concatenate([rv_s[rows(ri), col(h)], state_before[ri, h]], axis=0)
        o = _dot(lhs, rhs)
        ms = jnp.mean(o * o, axis=-1, keepdims=True)
        o = o * lax.rsqrt(ms + EPS)
        ycs = slice(ATT_WIDTH + h * RET_HEAD_DIM, ATT_WIDTH + (h + 1) * RET_HEAD_DIM)
        y_s[rows(ri), ycs] = (o * gr_s[rows(ri), col(h)]).astype(BF16)

    res = x_ref[0] + _dot(y_s[...], wout_ref[...])
    if final:
        ms = jnp.mean(res * res, axis=-1, keepdims=True)
        res = res * lax.rsqrt(ms + EPS) * fgain_ref[...]
    out_ref[0] = res

    kh_s[0:HIST, :] = kh_s[SEQ_BLOCK:SEQ_BLOCK + HIST, :]
    vt_s[:, 0:HIST] = vt_s[:, SEQ_BLOCK:SEQ_BLOCK + HIST]


def _const_spec(shape):
    zeros = (0,) * len(shape)
    return pl.BlockSpec(shape, lambda b, s: zeros)


def _layer(x, gain, win, wout, bias, cos, sin, dmat, qdec, kdec, bdec, fgain, *, final):
    batch, seq, _ = x.shape
    grid = (batch, seq // SEQ_BLOCK)
    blk = pl.BlockSpec((1, SEQ_BLOCK, D_MODEL), lambda b, s: (b, s, 0))
    tab = pl.BlockSpec((SEQ_BLOCK, LANES), lambda b, s: (s, 0))
    in_specs = [
        blk,
        _const_spec(gain.shape), _const_spec(win.shape),
        _const_spec(wout.shape), _const_spec(bias.shape), tab, tab,
        _const_spec(dmat.shape), _const_spec(qdec.shape), _const_spec(kdec.shape),
        _const_spec(bdec.shape), _const_spec(fgain.shape),
    ]
    scratch = [
        pltpu.VMEM((ATT_HEADS, ATT_KBLOCK, ATT_QBLOCK), F32),
        pltpu.VMEM((SEQ_BLOCK, D_MODEL), BF16),
        pltpu.VMEM((SEQ_BLOCK, ATT_WIDTH), BF16),
        pltpu.VMEM((HIST + SEQ_BLOCK, ATT_WIDTH), BF16),
        pltpu.VMEM((ATT_WIDTH, HIST + SEQ_BLOCK), BF16),
        pltpu.VMEM((SEQ_BLOCK, ATT_WIDTH), F32),
        pltpu.VMEM((SEQ_BLOCK, RET_WIDTH), BF16),
        pltpu.VMEM((SEQ_BLOCK, RET_WIDTH), BF16),
        pltpu.VMEM((SEQ_BLOCK, RET_WIDTH), BF16),
        pltpu.VMEM((SEQ_BLOCK, RET_WIDTH), BF16),
        pltpu.VMEM((SEQ_BLOCK, RET_WIDTH), BF16),
        pltpu.VMEM((SEQ_BLOCK, RET_WIDTH), F32),
        pltpu.VMEM((SEQ_BLOCK, MIX_WIDTH), BF16),
        pltpu.VMEM((RET_HEADS, RET_HEAD_DIM, RET_HEAD_DIM), F32),
    ]
    return pl.pallas_call(
        functools.partial(_layer_kernel, final=final),
        grid=grid,
        in_specs=in_specs,
        out_specs=blk,
        out_shape=jax.ShapeDtypeStruct(x.shape, x.dtype),
        scratch_shapes=scratch,
        compiler_params=pltpu.CompilerParams(
            dimension_semantics=("arbitrary", "arbitrary"),
            vmem_limit_bytes=VMEM_LIMIT_BYTES),
        name="hybrid_layer_final" if final else "hybrid_layer",
    )(x, gain, win, wout, bias, cos, sin, dmat, qdec, kdec, bdec, fgain)


def _attention_bias_vector(rel_bias):
    n = ATT_QBLOCK + ATT_KBLOCK - 1
    n_clipped = HIST + ATT_QBLOCK - 1 - REL_CLIP
    rb = rel_bias.astype(F32) * LOG2E
    return jnp.concatenate(
        [rb[:, 2 * REL_CLIP - (n - n_clipped) + 1:],
         jnp.broadcast_to(rb[:, 2 * REL_CLIP:], (ATT_HEADS, n_clipped + 1))], axis=1)


def _retention_tables():
    h = jnp.arange(RET_HEADS, dtype=F32)
    log_g = jnp.log1p(-jnp.exp2(-5.0 - h))
    idx = jnp.arange(RET_BLOCK, dtype=F32)
    chunk = jnp.arange(RET_BLOCK) // CHUNK
    causal = chunk[None, :] <= chunk[:, None]
    dmat = jnp.exp(jnp.abs(idx[:, None] - idx[None, :])[None] * log_g[:, None, None])
    dmat = jnp.where(causal[None], dmat, 0.0)
    qdec = jnp.exp((idx + 1.0)[None, :] * log_g[:, None])
    kdec = jnp.exp((RET_BLOCK - 1 - idx)[None, :] * log_g[:, None])
    bdec = jnp.exp(RET_BLOCK * log_g)
    wide = lambda t: jnp.broadcast_to(t[:, :, None], (RET_HEADS, RET_BLOCK, LANES))
    bdec = jnp.broadcast_to(bdec[:, None, None], (RET_HEADS, 1, LANES))
    return dmat, wide(qdec), wide(kdec), bdec


def _rotary_tables(positions):
    half = RET_HEAD_DIM // 2
    inv_freq = 1.0 / (ROPE_BASE ** jnp.linspace(0.0, 1.0, half, dtype=F32))
    ang = positions.astype(F32)[:, None] * inv_freq[None, :]
    cos, sin = jnp.cos(ang), jnp.sin(ang)
    return jnp.concatenate([cos, cos], axis=-1), jnp.concatenate([-sin, sin], axis=-1)


def kernel(x, positions, norm_gain, w_in, w_out, rel_bias, final_gain):
    depth = w_in.shape[0]
    cos, sin = _rotary_tables(positions)
    dmat, qdec, kdec, bdec = _retention_tables()
    fgain = final_gain.astype(F32)[None, :]
    win = w_in.astype(BF16)
    wout = w_out.astype(BF16)
    for layer in range(depth):
        x = _layer(
            x, norm_gain[layer].astype(F32)[None, :], win[layer], wout[layer],
            _attention_bias_vector(rel_bias[layer]), cos, sin, dmat, qdec, kdec, bdec, fgain,
            final=(layer == depth - 1))
    return x
```

```python
import functools

import jax
import jax.numpy as jnp
from jax import lax
from jax.experimental import pallas as pl
from jax.experimental.pallas import tpu as pltpu

D_MODEL = 1024
CHUNK = 64
ATT_HEADS = 8
ATT_HEAD_DIM = 64
ATT_WIDTH = ATT_HEADS * ATT_HEAD_DIM
LEFT_CHUNKS = 8
REL_CLIP = 256
RET_HEADS = 4
RET_HEAD_DIM = 128
RET_WIDTH = RET_HEADS * RET_HEAD_DIM
IN_WIDTH = 4 * ATT_WIDTH + 4 * RET_WIDTH
MIX_WIDTH = ATT_WIDTH + RET_WIDTH
EPS = 1e-6
ROPE_BASE = 10000.0
NEG_INF = -1e30
LOG2E = 1.4426950408889634

LANES = 128
SEQ_BLOCK = 512
HIST = LEFT_CHUNKS * CHUNK
NORM_ROWS = 64
PROJ_ROWS = 256
ATT_QBLOCK = 128
ATT_KBLOCK = HIST + ATT_QBLOCK
ATT_PAIRS = ATT_HEADS // 2
ATT_LOOKAHEAD = 1
RET_BLOCK = 128
VMEM_LIMIT_BYTES = 56 * 1024 * 1024

BF16 = jnp.bfloat16
F32 = jnp.float32

PIECE_AQ, PIECE_AK, PIECE_AV, PIECE_AG, PIECE_RQ, PIECE_RK, PIECE_RV, PIECE_RG = range(8)
PIECE_ORDER = (PIECE_AQ, PIECE_AG, PIECE_AK, PIECE_RQ, PIECE_AV, PIECE_RK, PIECE_RG, PIECE_RV)


def _dot(a, b):
    return jnp.dot(a, b, preferred_element_type=F32)


def _dot_nt(a, b):
    return lax.dot_general(a, b, (((1,), (1,)), ((), ())), preferred_element_type=F32)


def _dot_tn(a, b):
    return lax.dot_general(a, b, (((0,), (0,)), ((), ())), preferred_element_type=F32)


def _silu(g):
    return g * (1.0 / (1.0 + jnp.exp(-g)))


def _layer_kernel(x_ref, xnext_ref, gain_ref, win_ref, wout_ref, g_ref,
                  cos_ref, sin_ref, cosn_ref, sinn_ref,
                  dmat_ref, qdec_ref, kdec_ref, bdec_ref, fgain_ref,
                  out_ref,
                  bias_s, hn_s, q2_s, kh_s, kn_s, vt_s, vtn_s, ga2_s,
                  rq_s, rqd_s, rk_s, rkd_s, rv_s, gr_s, y_s, state_s,
                  *, final):
    b = pl.program_id(0)
    s = pl.program_id(1)
    first_step = jnp.logical_and(b == 0, s == 0)
    cur = (b * pl.num_programs(1) + s) % 2
    nxt = 1 - cur
    gain = gain_ref[...]
    n_rb = SEQ_BLOCK // RET_BLOCK
    rows = lambda ri: slice(ri * RET_BLOCK, (ri + 1) * RET_BLOCK)
    col = lambda h: slice(h * RET_HEAD_DIM, (h + 1) * RET_HEAD_DIM)

    def rms_norm_into_hn(src_ref):
        for r0 in range(0, SEQ_BLOCK, NORM_ROWS):
            xb = src_ref[0, r0:r0 + NORM_ROWS, :]
            ms = jnp.mean(xb * xb, axis=-1, keepdims=True)
            hn_s[r0:r0 + NORM_ROWS, :] = (xb * lax.rsqrt(ms + EPS) * gain).astype(BF16)

    def project(piece, r0, slot, k_dst, k_row0, vt_dst, vt_col0, cos_t, sin_t):
        rsl = slice(r0, r0 + PROJ_ROWS)
        acc = _dot(hn_s[rsl, :], win_ref[:, piece * ATT_WIDTH:(piece + 1) * ATT_WIDTH])
        if piece == PIECE_AQ:
            q2_s[slot, rsl, :] = (acc * (ATT_HEAD_DIM ** -0.5 * LOG2E)).astype(BF16)
        elif piece == PIECE_AK:
            k_dst[k_row0 + r0:k_row0 + r0 + PROJ_ROWS, :] = acc.astype(BF16)
        elif piece == PIECE_AV:
            vt_dst[:, vt_col0 + r0:vt_col0 + r0 + PROJ_ROWS] = acc.T.astype(BF16)
        elif piece == PIECE_AG:
            ga2_s[slot, rsl, :] = _silu(acc)
        elif piece == PIECE_RV:
            rv_s[rsl, :] = acc.astype(BF16)
        elif piece == PIECE_RG:
            gr_s[rsl, :] = _silu(acc)
        else:
            cos = cos_t[rsl, :]
            sin = sin_t[rsl, :]
            reps = PROJ_ROWS // RET_BLOCK
            for h in range(RET_HEADS):
                t = acc[:, col(h)]
                t = t * cos + pltpu.roll(t, RET_HEAD_DIM // 2, 1) * sin
                if piece == PIECE_RQ:
                    rq_s[rsl, col(h)] = t.astype(BF16)
                    dec = jnp.concatenate([qdec_ref[h]] * reps, axis=0)
                    rqd_s[rsl, col(h)] = (t * dec).astype(BF16)
                else:
                    t = t * (RET_HEAD_DIM ** -0.5)
                    rk_s[rsl, col(h)] = t.astype(BF16)
                    dec = jnp.concatenate([kdec_ref[h]] * reps, axis=0)
                    rkd_s[rsl, col(h)] = (t * dec).astype(BF16)

    proj_chunks = [(piece, r0) for piece in PIECE_ORDER for r0 in range(0, SEQ_BLOCK, PROJ_ROWS)]

    @pl.when(first_step)
    def _():
        row = lax.broadcasted_iota(jnp.int32, (ATT_QBLOCK, ATT_QBLOCK), 0)
        q_chunk = lax.broadcasted_iota(jnp.int32, (ATT_QBLOCK, ATT_QBLOCK), 1) // CHUNK
        for jb in range(0, ATT_KBLOCK, ATT_QBLOCK):
            k_chunk = (row + jb) // CHUNK
            in_band = jnp.logical_and(k_chunk >= q_chunk, k_chunk <= q_chunk + LEFT_CHUNKS)
            start = HIST - jb
            for h in range(ATT_HEADS):
                window = jnp.broadcast_to(g_ref[h:h + 1, start:start + 2 * ATT_QBLOCK],
                                          (ATT_QBLOCK, 2 * ATT_QBLOCK))
                rolled = pltpu.roll(window, ATT_QBLOCK + 1, 1, stride=1, stride_axis=0)
                bias_s[h, jb:jb + ATT_QBLOCK, :] = jnp.where(
                    in_band, rolled[:, :ATT_QBLOCK], NEG_INF)
        rms_norm_into_hn(x_ref)
        for piece, r0 in proj_chunks:
            project(piece, r0, cur, kh_s, HIST, vt_s, HIST, cos_ref, sin_ref)

    @pl.when(s == 0)
    def _():
        kh_s[0:HIST, :] = jnp.zeros((HIST, ATT_WIDTH), BF16)
        vt_s[:, 0:HIST] = jnp.zeros((ATT_WIDTH, HIST), BF16)
        state_s[...] = jnp.zeros(state_s.shape, F32)

    ret_items = [(ri, h) for ri in range(n_rb) for h in range(RET_HEADS)]
    kv = {(ri, h): _dot_tn(rkd_s[rows(ri), col(h)], rv_s[rows(ri), col(h)]) for ri, h in ret_items}
    att = {(ri, h): (_dot_nt(rq_s[rows(ri), col(h)], rk_s[rows(ri), col(h)]) * dmat_ref[h]).astype(BF16)
           for ri, h in ret_items}
    state_before = {}
    for h in range(RET_HEADS):
        st = state_s[h]
        for ri in range(n_rb):
            state_before[ri, h] = st.astype(BF16)
            st = st * bdec_ref[h] + kv[ri, h]
        state_s[h] = st
    for ri, h in ret_items:
        lhs = jnp.concatenate([att[ri, h], rqd_s[rows(ri), col(h)]], axis=1)
        rhs = jnp.concatenate([rv_s[rows(ri), col(h)], state_before[ri, h]], axis=0)
        o = _dot(lhs, rhs)
        ms = jnp.mean(o * o, axis=-1, keepdims=True)
        o = o * lax.rsqrt(ms + EPS)
        ycs = slice(ATT_WIDTH + h * RET_HEAD_DIM, ATT_WIDTH + (h + 1) * RET_HEAD_DIM)
        y_s[rows(ri), ycs] = (o * gr_s[rows(ri), col(h)]).astype(BF16)

    rms_norm_into_hn(xnext_ref)

    lane = lax.broadcasted_iota(jnp.int32, (ATT_QBLOCK, LANES), 1)
    low_half = lane < ATT_HEAD_DIM
    key_idx = lax.broadcasted_iota(jnp.int32, (ATT_KBLOCK, 2 * ATT_QBLOCK), 0)
    items = [(qi, hp) for qi in range(SEQ_BLOCK // ATT_QBLOCK) for hp in range(ATT_PAIRS)]
    assert len(items) == len(proj_chunks)
    q_cur = q2_s.at[cur]
    ga_cur = ga2_s.at[cur]

    def scores_t(qi, hp):
        r0 = qi * ATT_QBLOCK
        cs = slice(hp * LANES, (hp + 1) * LANES)
        qp = q_cur[r0:r0 + ATT_QBLOCK, cs]
        zero = jnp.zeros_like(qp)
        qq = jnp.concatenate([jnp.where(low_half, qp, zero), jnp.where(low_half, zero, qp)],
                             axis=0)
        bias = jnp.concatenate([bias_s[2 * hp], bias_s[2 * hp + 1]], axis=1)
        sc = _dot_nt(kh_s[r0:r0 + ATT_KBLOCK, cs], qq) + bias
        before_start = jnp.logical_and(s == 0, key_idx + r0 < HIST)
        return jnp.where(before_start, NEG_INF, sc)

    pending = {it: scores_t(*it) for it in items[:ATT_LOOKAHEAD]}
    for n, (qi, hp) in enumerate(items):
        if n + ATT_LOOKAHEAD < len(items):
            ahead = items[n + ATT_LOOKAHEAD]
            pending[ahead] = scores_t(*ahead)
        project(*proj_chunks[n], nxt, kn_s, 0, vtn_s, 0, cosn_ref, sinn_ref)
        sc = pending.pop((qi, hp))
        r0 = qi * ATT_QBLOCK
        cs = slice(hp * LANES, (hp + 1) * LANES)
        m = jnp.max(sc, axis=0, keepdims=True)
        e = jnp.exp2(sc - m)
        l = jnp.sum(e, axis=0, keepdims=True)
        ot = _dot(vt_s[cs, r0:r0 + ATT_KBLOCK], e.astype(BF16)) * (1.0 / l)
        o_pair_t = jnp.concatenate([ot[:ATT_HEAD_DIM, :ATT_QBLOCK],
                                    ot[ATT_HEAD_DIM:, ATT_QBLOCK:]], axis=0)
        y_s[r0:r0 + ATT_QBLOCK, cs] = (
            o_pair_t.T * ga_cur[r0:r0 + ATT_QBLOCK, cs]).astype(BF16)

    res = x_ref[0] + _dot(y_s[...], wout_ref[...])
    if final:
        ms = jnp.mean(res * res, axis=-1, keepdims=True)
        res = res * lax.rsqrt(ms + EPS) * fgain_ref[...]
    out_ref[0] = res

    kh_s[0:HIST, :] = kh_s[SEQ_BLOCK:SEQ_BLOCK + HIST, :]
    kh_s[HIST:HIST + SEQ_BLOCK, :] = kn_s[...]
    vt_s[:, 0:HIST] = vt_s[:, SEQ_BLOCK:SEQ_BLOCK + HIST]
    vt_s[:, HIST:HIST + SEQ_BLOCK] = vtn_s[...]


def _const_spec(shape):
    zeros = (0,) * len(shape)
    return pl.BlockSpec(shape, lambda b, s: zeros)


def _layer_spec(shape, layer):
    zeros = (0,) * (len(shape) - 1)
    return pl.BlockSpec((None,) + tuple(shape[1:]), lambda b, s: (layer,) + zeros)


def _layer(x, gain, win, wout, bias, cos, sin, dmat, qdec, kdec, bdec, fgain, *, layer, final):
    batch, seq, _ = x.shape
    n_seq = seq // SEQ_BLOCK
    grid = (batch, n_seq)

    def next_block(b, s):
        n = jnp.minimum(b * n_seq + s + 1, batch * n_seq - 1)
        return (n // n_seq, n % n_seq, 0)

    blk = pl.BlockSpec((1, SEQ_BLOCK, D_MODEL), lambda b, s: (b, s, 0))
    blk_next = pl.BlockSpec((1, SEQ_BLOCK, D_MODEL), next_block)
    tab = pl.BlockSpec((SEQ_BLOCK, LANES), lambda b, s: (s, 0))
    tab_next = pl.BlockSpec((SEQ_BLOCK, LANES), lambda b, s: next_block(b, s)[1:])
    in_specs = [
        blk, blk_next,
        _const_spec(gain.shape), _layer_spec(win.shape, layer),
        _layer_spec(wout.shape, layer), _const_spec(bias.shape), tab, tab, tab_next, tab_next,
        _const_spec(dmat.shape), _const_spec(qdec.shape), _const_spec(kdec.shape),
        _const_spec(bdec.shape), _const_spec(fgain.shape),
    ]
    scratch = [
        pltpu.VMEM((ATT_HEADS, ATT_KBLOCK, ATT_QBLOCK), F32),
        pltpu.VMEM((SEQ_BLOCK, D_MODEL), BF16),
        pltpu.VMEM((2, SEQ_BLOCK, ATT_WIDTH), BF16),
        pltpu.VMEM((HIST + SEQ_BLOCK, ATT_WIDTH), BF16),
        pltpu.VMEM((SEQ_BLOCK, ATT_WIDTH), BF16),
        pltpu.VMEM((ATT_WIDTH, HIST + SEQ_BLOCK), BF16),
        pltpu.VMEM((ATT_WIDTH, SEQ_BLOCK), BF16),
        pltpu.VMEM((2, SEQ_BLOCK, ATT_WIDTH), F32),
        pltpu.VMEM((SEQ_BLOCK, RET_WIDTH), BF16),
        pltpu.VMEM((SEQ_BLOCK, RET_WIDTH), BF16),
        pltpu.VMEM((SEQ_BLOCK, RET_WIDTH), BF16),
        pltpu.VMEM((SEQ_BLOCK, RET_WIDTH), BF16),
        pltpu.VMEM((SEQ_BLOCK, RET_WIDTH), BF16),
        pltpu.VMEM((SEQ_BLOCK, RET_WIDTH), F32),
        pltpu.VMEM((SEQ_BLOCK, MIX_WIDTH), BF16),
        pltpu.VMEM((RET_HEADS, RET_HEAD_DIM, RET_HEAD_DIM), F32),
    ]
    return pl.pallas_call(
        functools.partial(_layer_kernel, final=final),
        grid=grid,
        in_specs=in_specs,
        out_specs=blk,
        out_shape=jax.ShapeDtypeStruct(x.shape, x.dtype),
        scratch_shapes=scratch,
        compiler_params=pltpu.CompilerParams(
            dimension_semantics=("arbitrary", "arbitrary"),
            vmem_limit_bytes=VMEM_LIMIT_BYTES),
        name="hybrid_layer_final" if final else "hybrid_layer",
    )(x, x, gain, win, wout, bias, cos, sin, cos, sin, dmat, qdec, kdec, bdec, fgain)


def _attention_bias_vector(rel_bias):
    n = ATT_QBLOCK + ATT_KBLOCK - 1
    n_clipped = HIST + ATT_QBLOCK - 1 - REL_CLIP
    rb = rel_bias.astype(F32) * LOG2E
    return jnp.concatenate(
        [rb[:, 2 * REL_CLIP - (n - n_clipped) + 1:],
         jnp.broadcast_to(rb[:, 2 * REL_CLIP:], (ATT_HEADS, n_clipped + 1))], axis=1)


def _retention_tables():
    h = jnp.arange(RET_HEADS, dtype=F32)
    log_g = jnp.log1p(-jnp.exp2(-5.0 - h))
    idx = jnp.arange(RET_BLOCK, dtype=F32)
    chunk = jnp.arange(RET_BLOCK) // CHUNK
    causal = chunk[None, :] <= chunk[:, None]
    dmat = jnp.exp(jnp.abs(idx[:, None] - idx[None, :])[None] * log_g[:, None, None])
    dmat = jnp.where(causal[None], dmat, 0.0)
    qdec = jnp.exp((idx + 1.0)[None, :] * log_g[:, None])
    kdec = jnp.exp((RET_BLOCK - 1 - idx)[None, :] * log_g[:, None])
    bdec = jnp.exp(RET_BLOCK * log_g)
    wide = lambda t: jnp.broadcast_to(t[:, :, None], (RET_HEADS, RET_BLOCK, LANES))
    bdec = jnp.broadcast_to(bdec[:, None, None], (RET_HEADS, 1, LANES))
    return dmat, wide(qdec), wide(kdec), bdec


def _rotary_tables(positions):
    half = RET_HEAD_DIM // 2
    inv_freq = 1.0 / (ROPE_BASE ** jnp.linspace(0.0, 1.0, half, dtype=F32))
    ang = positions.astype(F32)[:, None] * inv_freq[None, :]
    cos, sin = jnp.cos(ang), jnp.sin(ang)
    return jnp.concatenate([cos, cos], axis=-1), jnp.concatenate([-sin, sin], axis=-1)


def kernel(x, positions, norm_gain, w_in, w_out, rel_bias, final_gain):
    depth = w_in.shape[0]
    cos, sin = _rotary_tables(positions)
    dmat, qdec, kdec, bdec = _retention_tables()
    fgain = final_gain.astype(F32)[None, :]
    win = w_in.astype(BF16)
    wout = w_out.astype(BF16)
    for layer in range(depth):
        x = _layer(
            x, norm_gain[layer].astype(F32)[None, :], win, wout,
            _attention_bias_vector(rel_bias[layer]), cos, sin, dmat, qdec, kdec, bdec, fgain,
            layer=layer, final=(layer == depth - 1))
    return x
```

```python
import functools

import jax
import jax.numpy as jnp
from jax import lax
from jax.experimental import pallas as pl
from jax.experimental.pallas import tpu as pltpu

D_MODEL = 1024
CHUNK = 64
ATT_HEADS = 8
ATT_HEAD_DIM = 64
ATT_WIDTH = ATT_HEADS * ATT_HEAD_DIM
LEFT_CHUNKS = 8
REL_CLIP = 256
RET_HEADS = 4
RET_HEAD_DIM = 128
RET_WIDTH = RET_HEADS * RET_HEAD_DIM
IN_WIDTH = 4 * ATT_WIDTH + 4 * RET_WIDTH
MIX_WIDTH = ATT_WIDTH + RET_WIDTH
EPS = 1e-6
ROPE_BASE = 10000.0
NEG_INF = -1e30
LOG2E = 1.4426950408889634

LANES = 128
SEQ_BLOCK = 512
HIST = LEFT_CHUNKS * CHUNK
NORM_ROWS = 64
PROJ_ROWS = 256
ATT_QBLOCK = 128
ATT_KBLOCK = HIST + ATT_QBLOCK
ATT_KBLOCKS = ATT_KBLOCK // ATT_QBLOCK
ATT_PAIRS = ATT_HEADS // 2
ATT_LOOKAHEAD = 2
RET_BLOCK = 128
VMEM_LIMIT_BYTES = 56 * 1024 * 1024

BF16 = jnp.bfloat16
F32 = jnp.float32

PIECE_AQ, PIECE_AK, PIECE_AV, PIECE_AG, PIECE_RQ, PIECE_RK, PIECE_RV, PIECE_RG = range(8)
PIECE_ORDER = (PIECE_AQ, PIECE_AG, PIECE_AK, PIECE_RQ, PIECE_AV, PIECE_RK, PIECE_RG, PIECE_RV)


def _dot(a, b):
    return jnp.dot(a, b, preferred_element_type=F32)


def _dot_nt(a, b):
    return lax.dot_general(a, b, (((1,), (1,)), ((), ())), preferred_element_type=F32)


def _dot_tn(a, b):
    return lax.dot_general(a, b, (((0,), (0,)), ((), ())), preferred_element_type=F32)


def _silu(g):
    return g * (1.0 / (1.0 + jnp.exp(-g)))


def _layer_kernel(x_ref, xnext_ref, gain_ref, win_ref, wout_ref, g_ref,
                  cos_ref, sin_ref, cosn_ref, sinn_ref,
                  dmat_ref, qdec_ref, kdec_ref, bdec_ref, fgain_ref,
                  out_ref,
                  bias_s, hn_s, q2_s, kh_s, kn_s, vt_s, vtn_s, ga2_s,
                  rq_s, rqd_s, rk_s, rkd_s, rv_s, gr_s, y_s, state_s,
                  *, final):
    b = pl.program_id(0)
    s = pl.program_id(1)
    first_step = jnp.logical_and(b == 0, s == 0)
    cur = (b * pl.num_programs(1) + s) % 2
    nxt = 1 - cur
    gain = gain_ref[...]
    n_rb = SEQ_BLOCK // RET_BLOCK
    rows = lambda ri: slice(ri * RET_BLOCK, (ri + 1) * RET_BLOCK)
    col = lambda h: slice(h * RET_HEAD_DIM, (h + 1) * RET_HEAD_DIM)

    def rms_norm_into_hn(src_ref):
        for r0 in range(0, SEQ_BLOCK, NORM_ROWS):
            xb = src_ref[0, r0:r0 + NORM_ROWS, :]
            ms = jnp.mean(xb * xb, axis=-1, keepdims=True)
            hn_s[r0:r0 + NORM_ROWS, :] = (xb * lax.rsqrt(ms + EPS) * gain).astype(BF16)

    def project(piece, r0, slot, k_dst, k_row0, vt_dst, vt_col0, cos_t, sin_t):
        rsl = slice(r0, r0 + PROJ_ROWS)
        acc = _dot(hn_s[rsl, :], win_ref[:, piece * ATT_WIDTH:(piece + 1) * ATT_WIDTH])
        if piece == PIECE_AQ:
            q2_s[slot, rsl, :] = (acc * (ATT_HEAD_DIM ** -0.5 * LOG2E)).astype(BF16)
        elif piece == PIECE_AK:
            k_dst[k_row0 + r0:k_row0 + r0 + PROJ_ROWS, :] = acc.astype(BF16)
        elif piece == PIECE_AV:
            vt_dst[:, vt_col0 + r0:vt_col0 + r0 + PROJ_ROWS] = acc.T.astype(BF16)
        elif piece == PIECE_AG:
            ga2_s[slot, rsl, :] = _silu(acc)
        elif piece == PIECE_RV:
            rv_s[rsl, :] = acc.astype(BF16)
        elif piece == PIECE_RG:
            gr_s[rsl, :] = _silu(acc)
        else:
            cos = cos_t[rsl, :]
            sin = sin_t[rsl, :]
            reps = PROJ_ROWS // RET_BLOCK
            for h in range(RET_HEADS):
                t = acc[:, col(h)]
                t = t * cos + pltpu.roll(t, RET_HEAD_DIM // 2, 1) * sin
                if piece == PIECE_RQ:
                    rq_s[rsl, col(h)] = t.astype(BF16)
                    dec = jnp.concatenate([qdec_ref[h]] * reps, axis=0)
                    rqd_s[rsl, col(h)] = (t * dec).astype(BF16)
                else:
                    t = t * (RET_HEAD_DIM ** -0.5)
                    rk_s[rsl, col(h)] = t.astype(BF16)
                    dec = jnp.concatenate([kdec_ref[h]] * reps, axis=0)
                    rkd_s[rsl, col(h)] = (t * dec).astype(BF16)

    proj_chunks = [(piece, r0) for piece in PIECE_ORDER for r0 in range(0, SEQ_BLOCK, PROJ_ROWS)]

    @pl.when(first_step)
    def _():
        row = lax.broadcasted_iota(jnp.int32, (ATT_QBLOCK, ATT_QBLOCK), 0)
        q_chunk = lax.broadcasted_iota(jnp.int32, (ATT_QBLOCK, ATT_QBLOCK), 1) // CHUNK
        for jb in range(ATT_KBLOCKS):
            k_chunk = (row + jb * ATT_QBLOCK) // CHUNK
            in_band = jnp.logical_and(k_chunk >= q_chunk, k_chunk <= q_chunk + LEFT_CHUNKS)
            start = HIST - jb * ATT_QBLOCK
            for h in range(ATT_HEADS):
                window = jnp.broadcast_to(g_ref[h:h + 1, start:start + 2 * ATT_QBLOCK],
                                          (ATT_QBLOCK, 2 * ATT_QBLOCK))
                rolled = pltpu.roll(window, ATT_QBLOCK + 1, 1, stride=1, stride_axis=0)
                bias_s[h, jb] = jnp.where(in_band, rolled[:, :ATT_QBLOCK], NEG_INF)
        for h in range(ATT_HEADS):
            bias_s[h, ATT_KBLOCKS] = jnp.full((ATT_QBLOCK, ATT_QBLOCK), NEG_INF, F32)
        rms_norm_into_hn(x_ref)
        for piece, r0 in proj_chunks:
            project(piece, r0, cur, kh_s, HIST, vt_s, HIST, cos_ref, sin_ref)

    @pl.when(s == 0)
    def _():
        kh_s[0:HIST, :] = jnp.zeros((HIST, ATT_WIDTH), BF16)
        vt_s[:, 0:HIST] = jnp.zeros((ATT_WIDTH, HIST), BF16)
        state_s[...] = jnp.zeros(state_s.shape, F32)

    ret_items = [(ri, h) for ri in range(n_rb) for h in range(RET_HEADS)]
    kv = {(ri, h): _dot_tn(rkd_s[rows(ri), col(h)], rv_s[rows(ri), col(h)]) for ri, h in ret_items}
    att = {(ri, h): (_dot_nt(rq_s[rows(ri), col(h)], rk_s[rows(ri), col(h)]) * dmat_ref[h]).astype(BF16)
           for ri, h in ret_items}
    state_before = {}
    for h in range(RET_HEADS):
        st = state_s[h]
        for ri in range(n_rb):
            state_before[ri, h] = st.astype(BF16)
            st = st * bdec_ref[h] + kv[ri, h]
        state_s[h] = st
    for ri, h in ret_items:
        lhs = jnp.concatenate([att[ri, h], rqd_s[rows(ri), col(h)]], axis=1)
        rhs = jnp.concatenate([rv_s[rows(ri), col(h)], state_before[ri, h]], axis=0)
        o = _dot(lhs, rhs)
        ms = jnp.mean(o * o, axis=-1, keepdims=True)
        o = o * lax.rsqrt(ms + EPS)
        ycs = slice(ATT_WIDTH + h * RET_HEAD_DIM, ATT_WIDTH + (h + 1) * RET_HEAD_DIM)
        y_s[rows(ri), ycs] = (o * gr_s[rows(ri), col(h)]).astype(BF16)

    rms_norm_into_hn(xnext_ref)

    lane = lax.broadcasted_iota(jnp.int32, (ATT_QBLOCK, LANES), 1)
    low_half = lane < ATT_HEAD_DIM
    items = [(qi, hp) for qi in range(SEQ_BLOCK // ATT_QBLOCK) for hp in range(ATT_PAIRS)]
    assert len(items) == len(proj_chunks)
    q_cur = q2_s.at[cur]
    ga_cur = ga2_s.at[cur]

    def scores_t(qi, hp):
        r0 = qi * ATT_QBLOCK
        cs = slice(hp * LANES, (hp + 1) * LANES)
        qp = q_cur[r0:r0 + ATT_QBLOCK, cs]
        zero = jnp.zeros_like(qp)
        qq = jnp.concatenate([jnp.where(low_half, qp, zero), jnp.where(low_half, zero, qp)],
                             axis=0)
        n_before = jnp.where(s == 0, HIST // ATT_QBLOCK - qi, 0)
        blocks = [jnp.where(jb < n_before, ATT_KBLOCKS, jb) for jb in range(ATT_KBLOCKS)]
        bias = jnp.concatenate(
            [jnp.concatenate([bias_s[h, blk] for blk in blocks], axis=0)
             for h in (2 * hp, 2 * hp + 1)], axis=1)
        return _dot_nt(kh_s[r0:r0 + ATT_KBLOCK, cs], qq) + bias

    pending = {it: scores_t(*it) for it in items[:ATT_LOOKAHEAD]}
    for n, (qi, hp) in enumerate(items):
        if n + ATT_LOOKAHEAD < len(items):
            ahead = items[n + ATT_LOOKAHEAD]
            pending[ahead] = scores_t(*ahead)
        project(*proj_chunks[n], nxt, kn_s, 0, vtn_s, 0, cosn_ref, sinn_ref)
        sc = pending.pop((qi, hp))
        r0 = qi * ATT_QBLOCK
        cs = slice(hp * LANES, (hp + 1) * LANES)
        m = jnp.max(sc, axis=0, keepdims=True)
        e = jnp.exp2(sc - m)
        l = jnp.sum(e, axis=0, keepdims=True)
        ot = _dot(vt_s[cs, r0:r0 + ATT_KBLOCK], e.astype(BF16)) * (1.0 / l)
        o_pair_t = jnp.concatenate([ot[:ATT_HEAD_DIM, :ATT_QBLOCK],
                                    ot[ATT_HEAD_DIM:, ATT_QBLOCK:]], axis=0)
        y_s[r0:r0 + ATT_QBLOCK, cs] = (
            o_pair_t.T * ga_cur[r0:r0 + ATT_QBLOCK, cs]).astype(BF16)

    res = x_ref[0] + _dot(y_s[...], wout_ref[...])
    if final:
        ms = jnp.mean(res * res, axis=-1, keepdims=True)
        res = res * lax.rsqrt(ms + EPS) * fgain_ref[...]
    out_ref[0] = res

    kh_s[0:HIST, :] = kh_s[SEQ_BLOCK:SEQ_BLOCK + HIST, :]
    kh_s[HIST:HIST + SEQ_BLOCK, :] = kn_s[...]
    vt_s[:, 0:HIST] = vt_s[:, SEQ_BLOCK:SEQ_BLOCK + HIST]
    vt_s[:, HIST:HIST + SEQ_BLOCK] = vtn_s[...]


def _const_spec(shape):
    zeros = (0,) * len(shape)
    return pl.BlockSpec(shape, lambda b, s: zeros)


def _layer_spec(shape, layer):
    zeros = (0,) * (len(shape) - 1)
    return pl.BlockSpec((None,) + tuple(shape[1:]), lambda b, s: (layer,) + zeros)


def _layer(x, gain, win, wout, bias, cos, sin, dmat, qdec, kdec, bdec, fgain, *, layer, final):
    batch, seq, _ = x.shape
    n_seq = seq // SEQ_BLOCK
    grid = (batch, n_seq)

    def next_block(b, s):
        n = jnp.minimum(b * n_seq + s + 1, batch * n_seq - 1)
        return (n // n_seq, n % n_seq, 0)

    blk = pl.BlockSpec((1, SEQ_BLOCK, D_MODEL), lambda b, s: (b, s, 0))
    blk_next = pl.BlockSpec((1, SEQ_BLOCK, D_MODEL), next_block)
    tab = pl.BlockSpec((SEQ_BLOCK, LANES), lambda b, s: (s, 0))
    tab_next = pl.BlockSpec((SEQ_BLOCK, LANES), lambda b, s: next_block(b, s)[1:])
    in_specs = [
        blk, blk_next,
        _const_spec(gain.shape), _layer_spec(win.shape, layer),
        _layer_spec(wout.shape, layer), _const_spec(bias.shape), tab, tab, tab_next, tab_next,
        _const_spec(dmat.shape), _const_spec(qdec.shape), _const_spec(kdec.shape),
        _const_spec(bdec.shape), _const_spec(fgain.shape),
    ]
    scratch = [
        pltpu.VMEM((ATT_HEADS, ATT_KBLOCKS + 1, ATT_QBLOCK, ATT_QBLOCK), F32),
        pltpu.VMEM((SEQ_BLOCK, D_MODEL), BF16),
        pltpu.VMEM((2, SEQ_BLOCK, ATT_WIDTH), BF16),
        pltpu.VMEM((HIST + SEQ_BLOCK, ATT_WIDTH), BF16),
        pltpu.VMEM((SEQ_BLOCK, ATT_WIDTH), BF16),
        pltpu.VMEM((ATT_WIDTH, HIST + SEQ_BLOCK), BF16),
        pltpu.VMEM((ATT_WIDTH, SEQ_BLOCK), BF16),
        pltpu.VMEM((2, SEQ_BLOCK, ATT_WIDTH), F32),
        pltpu.VMEM((SEQ_BLOCK, RET_WIDTH), BF16),
        pltpu.VMEM((SEQ_BLOCK, RET_WIDTH), BF16),
        pltpu.VMEM((SEQ_BLOCK, RET_WIDTH), BF16),
        pltpu.VMEM((SEQ_BLOCK, RET_WIDTH), BF16),
        pltpu.VMEM((SEQ_BLOCK, RET_WIDTH), BF16),
        pltpu.VMEM((SEQ_BLOCK, RET_WIDTH), F32),
        pltpu.VMEM((SEQ_BLOCK, MIX_WIDTH), BF16),
        pltpu.VMEM((RET_HEADS, RET_HEAD_DIM, RET_HEAD_DIM), F32),
    ]
    return pl.pallas_call(
        functools.partial(_layer_kernel, final=final),
        grid=grid,
        in_specs=in_specs,
        out_specs=blk,
        out_shape=jax.ShapeDtypeStruct(x.shape, x.dtype),
        scratch_shapes=scratch,
        compiler_params=pltpu.CompilerParams(
            dimension_semantics=("arbitrary", "arbitrary"),
            vmem_limit_bytes=VMEM_LIMIT_BYTES),
        name="hybrid_layer_final" if final else "hybrid_layer",
    )(x, x, gain, win, wout, bias, cos, sin, cos, sin, dmat, qdec, kdec, bdec, fgain)


def _attention_bias_vector(rel_bias):
    n = ATT_QBLOCK + ATT_KBLOCK - 1
    n_clipped = HIST + ATT_QBLOCK - 1 - REL_CLIP
    rb = rel_bias.astype(F32) * LOG2E
    return jnp.concatenate(
        [rb[:, 2 * REL_CLIP - (n - n_clipped) + 1:],
         jnp.broadcast_to(rb[:, 2 * REL_CLIP:], (ATT_HEADS, n_clipped + 1))], axis=1)


def _retention_tables():
    h = jnp.arange(RET_HEADS, dtype=F32)
    log_g = jnp.log1p(-jnp.exp2(-5.0 - h))
    idx = jnp.arange(RET_BLOCK, dtype=F32)
    chunk = jnp.arange(RET_BLOCK) // CHUNK
    causal = chunk[None, :] <= chunk[:, None]
    dmat = jnp.exp(jnp.abs(idx[:, None] - idx[None, :])[None] * log_g[:, None, None])
    dmat = jnp.where(causal[None], dmat, 0.0)
    qdec = jnp.exp((idx + 1.0)[None, :] * log_g[:, None])
    kdec = jnp.exp((RET_BLOCK - 1 - idx)[None, :] * log_g[:, None])
    bdec = jnp.exp(RET_BLOCK * log_g)
    wide = lambda t: jnp.broadcast_to(t[:, :, None], (RET_HEADS, RET_BLOCK, LANES))
    bdec = jnp.broadcast_to(bdec[:, None, None], (RET_HEADS, 1, LANES))
    return dmat, wide(qdec), wide(kdec), bdec


def _rotary_tables(positions):
    half = RET_HEAD_DIM // 2
    inv_freq = 1.0 / (ROPE_BASE ** jnp.linspace(0.0, 1.0, half, dtype=F32))
    ang = positions.astype(F32)[:, None] * inv_freq[None, :]
    cos, sin = jnp.cos(ang), jnp.sin(ang)
    return jnp.concatenate([cos, cos], axis=-1), jnp.concatenate([-sin, sin], axis=-1)


def kernel(x, positions, norm_gain, w_in, w_out, rel_bias, final_gain):
    depth = w_in.shape[0]
    cos, sin = _rotary_tables(positions)
    dmat, qdec, kdec, bdec = _retention_tables()
    fgain = final_gain.astype(F32)[None, :]
    win = w_in.astype(BF16)
    wout = w_out.astype(BF16)
    for layer in range(depth):
        x = _layer(
            x, norm_gain[layer].astype(F32)[None, :], win, wout,
            _attention_bias_vector(rel_bias[layer]), cos, sin, dmat, qdec, kdec, bdec, fgain,
            layer=layer, final=(layer == depth - 1))
    return x
```

```python
import functools

import jax
import jax.numpy as jnp
from jax import lax
from jax.experimental import pallas as pl
from jax.experimental.pallas import tpu as pltpu

D_MODEL = 1024
CHUNK = 64
ATT_HEADS = 8
ATT_HEAD_DIM = 64
ATT_WIDTH = ATT_HEADS * ATT_HEAD_DIM
LEFT_CHUNKS = 8
REL_CLIP = 256
RET_HEADS = 4
RET_HEAD_DIM = 128
RET_WIDTH = RET_HEADS * RET_HEAD_DIM
IN_WIDTH = 4 * ATT_WIDTH + 4 * RET_WIDTH
MIX_WIDTH = ATT_WIDTH + RET_WIDTH
EPS = 1e-6
ROPE_BASE = 10000.0
NEG_INF = -1e30
LOG2E = 1.4426950408889634

LANES = 128
SEQ_BLOCK = 512
HIST = LEFT_CHUNKS * CHUNK
NORM_ROWS = 64
PROJ_ROWS = 256
ATT_QBLOCK = 128
ATT_KBLOCK = HIST + ATT_QBLOCK
ATT_KBLOCKS = ATT_KBLOCK // ATT_QBLOCK
ATT_PAIRS = ATT_HEADS // 2
ATT_LOOKAHEAD = 2
RET_BLOCK = 128
VMEM_LIMIT_BYTES = 56 * 1024 * 1024

BF16 = jnp.bfloat16
F32 = jnp.float32

PIECE_AQ, PIECE_AK, PIECE_AV, PIECE_AG, PIECE_RQ, PIECE_RK, PIECE_RV, PIECE_RG = range(8)
PIECE_ORDER = (PIECE_AQ, PIECE_AG, PIECE_AK, PIECE_RQ, PIECE_AV, PIECE_RK, PIECE_RG, PIECE_RV)


def _dot(a, b):
    return jnp.dot(a, b, preferred_element_type=F32)


def _dot_nt(a, b):
    return lax.dot_general(a, b, (((1,), (1,)), ((), ())), preferred_element_type=F32)


def _dot_tn(a, b):
    return lax.dot_general(a, b, (((0,), (0,)), ((), ())), preferred_element_type=F32)


def _silu(g):
    return g * (1.0 / (1.0 + jnp.exp(-g)))


def _layer_kernel(x_ref, xnext_ref, gain_ref, win_ref, wout_ref, g_ref,
                  cos_ref, sin_ref, cosn_ref, sinn_ref,
                  dmat_ref, qdec_ref, kdec_ref, bdec_ref, fgain_ref,
                  out_ref,
                  bias_s, hn_s, q2_s, kh_s, kn_s, vt_s, vtn_s, ga2_s,
                  rq_s, rqd_s, rk_s, rkd_s, rv_s, gr_s, y_s, state_s,
                  *, final):
    b = pl.program_id(0)
    s = pl.program_id(1)
    first_step = jnp.logical_and(b == 0, s == 0)
    cur = (b * pl.num_programs(1) + s) % 2
    nxt = 1 - cur
    gain = gain_ref[...]
    n_rb = SEQ_BLOCK // RET_BLOCK
    rows = lambda ri: slice(ri * RET_BLOCK, (ri + 1) * RET_BLOCK)
    col = lambda h: slice(h * RET_HEAD_DIM, (h + 1) * RET_HEAD_DIM)

    def rms_norm_rows(src_ref, r0, n):
        xb = src_ref[0, r0:r0 + n, :]
        ms = jnp.mean(xb * xb, axis=-1, keepdims=True)
        hn = (xb * lax.rsqrt(ms + EPS) * gain).astype(BF16)
        hn_s[r0:r0 + n, :] = hn
        return hn

    def rms_norm_into_hn(src_ref):
        for r0 in range(0, SEQ_BLOCK, NORM_ROWS):
            rms_norm_rows(src_ref, r0, NORM_ROWS)


    def project(piece, r0, slot, k_dst, k_row0, vt_dst, vt_col0, cos_t, sin_t):
        rsl = slice(r0, r0 + PROJ_ROWS)
        acc = _dot(hn_s[rsl, :], win_ref[:, piece * ATT_WIDTH:(piece + 1) * ATT_WIDTH])
        if piece == PIECE_AQ:
            q2_s[slot, rsl, :] = (acc * (ATT_HEAD_DIM ** -0.5 * LOG2E)).astype(BF16)
        elif piece == PIECE_AK:
            k_dst[k_row0 + r0:k_row0 + r0 + PROJ_ROWS, :] = acc.astype(BF16)
        elif piece == PIECE_AV:
            vt_dst[:, vt_col0 + r0:vt_col0 + r0 + PROJ_ROWS] = acc.T.astype(BF16)
        elif piece == PIECE_AG:
            ga2_s[slot, rsl, :] = _silu(acc)
        elif piece == PIECE_RV:
            rv_s[rsl, :] = acc.astype(BF16)
        elif piece == PIECE_RG:
            gr_s[rsl, :] = _silu(acc)
        else:
            cos = cos_t[rsl, :]
            sin = sin_t[rsl, :]
            reps = PROJ_ROWS // RET_BLOCK
            for h in range(RET_HEADS):
                t = acc[:, col(h)]
                t = t * cos + pltpu.roll(t, RET_HEAD_DIM // 2, 1) * sin
                if piece == PIECE_RQ:
                    rq_s[rsl, col(h)] = t.astype(BF16)
                    dec = jnp.concatenate([qdec_ref[h]] * reps, axis=0)
                    rqd_s[rsl, col(h)] = (t * dec).astype(BF16)
                else:
                    t = t * (RET_HEAD_DIM ** -0.5)
                    rk_s[rsl, col(h)] = t.astype(BF16)
                    dec = jnp.concatenate([kdec_ref[h]] * reps, axis=0)
                    rkd_s[rsl, col(h)] = (t * dec).astype(BF16)

    proj_chunks = [(piece, r0) for piece in PIECE_ORDER for r0 in range(0, SEQ_BLOCK, PROJ_ROWS)]

    @pl.when(first_step)
    def _():
        row = lax.broadcasted_iota(jnp.int32, (ATT_QBLOCK, ATT_QBLOCK), 0)
        q_chunk = lax.broadcasted_iota(jnp.int32, (ATT_QBLOCK, ATT_QBLOCK), 1) // CHUNK
        for jb in range(ATT_KBLOCKS):
            k_chunk = (row + jb * ATT_QBLOCK) // CHUNK
            in_band = jnp.logical_and(k_chunk >= q_chunk, k_chunk <= q_chunk + LEFT_CHUNKS)
            start = HIST - jb * ATT_QBLOCK
            for h in range(ATT_HEADS):
                window = jnp.broadcast_to(g_ref[h:h + 1, start:start + 2 * ATT_QBLOCK],
                                          (ATT_QBLOCK, 2 * ATT_QBLOCK))
                rolled = pltpu.roll(window, ATT_QBLOCK + 1, 1, stride=1, stride_axis=0)
                bias_s[h, jb] = jnp.where(in_band, rolled[:, :ATT_QBLOCK], NEG_INF)
        for h in range(ATT_HEADS):
            bias_s[h, ATT_KBLOCKS] = jnp.full((ATT_QBLOCK, ATT_QBLOCK), NEG_INF, F32)
        rms_norm_into_hn(x_ref)
        for piece, r0 in proj_chunks:
            project(piece, r0, cur, kh_s, HIST, vt_s, HIST, cos_ref, sin_ref)

    @pl.when(s == 0)
    def _():
        kh_s[0:HIST, :] = jnp.zeros((HIST, ATT_WIDTH), BF16)
        vt_s[:, 0:HIST] = jnp.zeros((ATT_WIDTH, HIST), BF16)
        state_s[...] = jnp.zeros(state_s.shape, F32)

    lane = lax.broadcasted_iota(jnp.int32, (ATT_QBLOCK, LANES), 1)
    low_half = lane < ATT_HEAD_DIM
    items = [(qi, hp) for qi in range(SEQ_BLOCK // ATT_QBLOCK) for hp in range(ATT_PAIRS)]
    items_per_chunk = len(items) // len(proj_chunks)
    assert len(items) == items_per_chunk * len(proj_chunks)
    q_cur = q2_s.at[cur]
    ga_cur = ga2_s.at[cur]

    def scores_t(qi, hp):
        r0 = qi * ATT_QBLOCK
        cs = slice(hp * LANES, (hp + 1) * LANES)
        qp = q_cur[r0:r0 + ATT_QBLOCK, cs]
        zero = jnp.zeros_like(qp)
        qq = jnp.concatenate([jnp.where(low_half, qp, zero), jnp.where(low_half, zero, qp)],
                             axis=0)
        n_before = jnp.where(s == 0, HIST // ATT_QBLOCK - qi, 0)
        blocks = [jnp.where(jb < n_before, ATT_KBLOCKS, jb) for jb in range(ATT_KBLOCKS)]
        bias = jnp.concatenate(
            [jnp.concatenate([bias_s[h, blk] for blk in blocks], axis=0)
             for h in (2 * hp, 2 * hp + 1)], axis=1)
        return _dot_nt(kh_s[r0:r0 + ATT_KBLOCK, cs], qq) + bias

    pending = {it: scores_t(*it) for it in items[:ATT_LOOKAHEAD]}

    ret_items = [(ri, h) for ri in range(n_rb) for h in range(RET_HEADS)]
    kv = {(ri, h): _dot_tn(rkd_s[rows(ri), col(h)], rv_s[rows(ri), col(h)]) for ri, h in ret_items}
    att = {(ri, h): (_dot_nt(rq_s[rows(ri), col(h)], rk_s[rows(ri), col(h)]) * dmat_ref[h]).astype(BF16)
           for ri, h in ret_items}
    state_before = {}
    for h in range(RET_HEADS):
        st = state_s[h]
        for ri in range(n_rb):
            state_before[ri, h] = st.astype(BF16)
            st = st * bdec_ref[h] + kv[ri, h]
        state_s[h] = st
    for ri, h in ret_items:
        lhs = jnp.concatenate([att[ri, h], rqd_s[rows(ri), col(h)]], axis=1)
        rhs = jnp.concatenate([rv_s[rows(ri), col(h)], state_before[ri, h]], axis=0)
        o = _dot(lhs, rhs)
        ms = jnp.mean(o * o, axis=-1, keepdims=True)
        o = o * lax.rsqrt(ms + EPS)
        ycs = slice(ATT_WIDTH + h * RET_HEAD_DIM, ATT_WIDTH + (h + 1) * RET_HEAD_DIM)
        y_s[rows(ri), ycs] = (o * gr_s[rows(ri), col(h)]).astype(BF16)

    rms_norm_into_hn(xnext_ref)

    for n, (qi, hp) in enumerate(items):
        if n + ATT_LOOKAHEAD < len(items):
            ahead = items[n + ATT_LOOKAHEAD]
            pending[ahead] = scores_t(*ahead)
        if n % items_per_chunk == 0:
            project(*proj_chunks[n // items_per_chunk], nxt, kn_s, 0, vtn_s, 0, cosn_ref, sinn_ref)
        sc = pending.pop((qi, hp))
        r0 = qi * ATT_QBLOCK
        cs = slice(hp * LANES, (hp + 1) * LANES)
        m = jnp.max(sc, axis=0, keepdims=True)
        e = jnp.exp2(sc - m)
        l = jnp.sum(e, axis=0, keepdims=True)
        ot = _dot(vt_s[cs, r0:r0 + ATT_KBLOCK], e.astype(BF16)) * (1.0 / l)
        o_pair_t = jnp.concatenate([ot[:ATT_HEAD_DIM, :ATT_QBLOCK],
                                    ot[ATT_HEAD_DIM:, ATT_QBLOCK:]], axis=0)
        y_s[r0:r0 + ATT_QBLOCK, cs] = (
            o_pair_t.T * ga_cur[r0:r0 + ATT_QBLOCK, cs]).astype(BF16)

    res = x_ref[0] + _dot(y_s[...], wout_ref[...])
    if final:
        ms = jnp.mean(res * res, axis=-1, keepdims=True)
        res = res * lax.rsqrt(ms + EPS) * fgain_ref[...]
    out_ref[0] = res

    kh_s[0:HIST, :] = kh_s[SEQ_BLOCK:SEQ_BLOCK + HIST, :]
    kh_s[HIST:HIST + SEQ_BLOCK, :] = kn_s[...]
    vt_s[:, 0:HIST] = vt_s[:, SEQ_BLOCK:SEQ_BLOCK + HIST]
    vt_s[:, HIST:HIST + SEQ_BLOCK] = vtn_s[...]


def _const_spec(shape):
    zeros = (0,) * len(shape)
    return pl.BlockSpec(shape, lambda b, s: zeros)


def _layer_spec(shape, layer):
    zeros = (0,) * (len(shape) - 1)
    return pl.BlockSpec((None,) + tuple(shape[1:]), lambda b, s: (layer,) + zeros)


def _layer(x, gain, win, wout, bias, cos, sin, dmat, qdec, kdec, bdec, fgain, *, layer, final):
    batch, seq, _ = x.shape
    n_seq = seq // SEQ_BLOCK
    grid = (batch, n_seq)

    def next_block(b, s):
        n = jnp.minimum(b * n_seq + s + 1, batch * n_seq - 1)
        return (n // n_seq, n % n_seq, 0)

    blk = pl.BlockSpec((1, SEQ_BLOCK, D_MODEL), lambda b, s: (b, s, 0))
    blk_next = pl.BlockSpec((1, SEQ_BLOCK, D_MODEL), next_block)
    tab = pl.BlockSpec((SEQ_BLOCK, LANES), lambda b, s: (s, 0))
    tab_next = pl.BlockSpec((SEQ_BLOCK, LANES), lambda b, s: next_block(b, s)[1:])
    in_specs = [
        blk, blk_next,
        _const_spec(gain.shape), _layer_spec(win.shape, layer),
        _layer_spec(wout.shape, layer), _const_spec(bias.shape), tab, tab, tab_next, tab_next,
        _const_spec(dmat.shape), _const_spec(qdec.shape), _const_spec(kdec.shape),
        _const_spec(bdec.shape), _const_spec(fgain.shape),
    ]
    scratch = [
        pltpu.VMEM((ATT_HEADS, ATT_KBLOCKS + 1, ATT_QBLOCK, ATT_QBLOCK), F32),
        pltpu.VMEM((SEQ_BLOCK, D_MODEL), BF16),
        pltpu.VMEM((2, SEQ_BLOCK, ATT_WIDTH), BF16),
        pltpu.VMEM((HIST + SEQ_BLOCK, ATT_WIDTH), BF16),
        pltpu.VMEM((SEQ_BLOCK, ATT_WIDTH), BF16),
        pltpu.VMEM((ATT_WIDTH, HIST + SEQ_BLOCK), BF16),
        pltpu.VMEM((ATT_WIDTH, SEQ_BLOCK), BF16),
        pltpu.VMEM((2, SEQ_BLOCK, ATT_WIDTH), F32),
        pltpu.VMEM((SEQ_BLOCK, RET_WIDTH), BF16),
        pltpu.VMEM((SEQ_BLOCK, RET_WIDTH), BF16),
        pltpu.VMEM((SEQ_BLOCK, RET_WIDTH), BF16),
        pltpu.VMEM((SEQ_BLOCK, RET_WIDTH), BF16),
        pltpu.VMEM((SEQ_BLOCK, RET_WIDTH), BF16),
        pltpu.VMEM((SEQ_BLOCK, RET_WIDTH), F32),
        pltpu.VMEM((SEQ_BLOCK, MIX_WIDTH), BF16),
        pltpu.VMEM((RET_HEADS, RET_HEAD_DIM, RET_HEAD_DIM), F32),
    ]
    return pl.pallas_call(
        functools.partial(_layer_kernel, final=final),
        grid=grid,
        in_specs=in_specs,
        out_specs=blk,
        out_shape=jax.ShapeDtypeStruct(x.shape, x.dtype),
        scratch_shapes=scratch,
        compiler_params=pltpu.CompilerParams(
            dimension_semantics=("arbitrary", "arbitrary"),
            vmem_limit_bytes=VMEM_LIMIT_BYTES),
        name="hybrid_layer_final" if final else "hybrid_layer",
    )(x, x, gain, win, wout, bias, cos, sin, cos, sin, dmat, qdec, kdec, bdec, fgain)


def _attention_bias_vector(rel_bias):
    n = ATT_QBLOCK + ATT_KBLOCK - 1
    n_clipped = HIST + ATT_QBLOCK - 1 - REL_CLIP
    rb = rel_bias.astype(F32) * LOG2E
    return jnp.concatenate(
        [rb[:, 2 * REL_CLIP - (n - n_clipped) + 1:],
         jnp.broadcast_to(rb[:, 2 * REL_CLIP:], (ATT_HEADS, n_clipped + 1))], axis=1)


def _retention_tables():
    h = jnp.arange(RET_HEADS, dtype=F32)
    log_g = jnp.log1p(-jnp.exp2(-5.0 - h))
    idx = jnp.arange(RET_BLOCK, dtype=F32)
    chunk = jnp.arange(RET_BLOCK) // CHUNK
    causal = chunk[None, :] <= chunk[:, None]
    dmat = jnp.exp(jnp.abs(idx[:, None] - idx[None, :])[None] * log_g[:, None, None])
    dmat = jnp.where(causal[None], dmat, 0.0)
    qdec = jnp.exp((idx + 1.0)[None, :] * log_g[:, None])
    kdec = jnp.exp((RET_BLOCK - 1 - idx)[None, :] * log_g[:, None])
    bdec = jnp.exp(RET_BLOCK * log_g)
    wide = lambda t: jnp.broadcast_to(t[:, :, None], (RET_HEADS, RET_BLOCK, LANES))
    bdec = jnp.broadcast_to(bdec[:, None, None], (RET_HEADS, 1, LANES))
    return dmat, wide(qdec), wide(kdec), bdec


def _rotary_tables(positions):
    half = RET_HEAD_DIM // 2
    inv_freq = 1.0 / (ROPE_BASE ** jnp.linspace(0.0, 1.0, half, dtype=F32))
    ang = positions.astype(F32)[:, None] * inv_freq[None, :]
    cos, sin = jnp.cos(ang), jnp.sin(ang)
    return jnp.concatenate([cos, cos], axis=-1), jnp.concatenate([-sin, sin], axis=-1)


def kernel(x, positions, norm_gain, w_in, w_out, rel_bias, final_gain):
    depth = w_in.shape[0]
    cos, sin = _rotary_tables(positions)
    dmat, qdec, kdec, bdec = _retention_tables()
    fgain = final_gain.astype(F32)[None, :]
    win = w_in.astype(BF16)
    wout = w_out.astype(BF16)
    for layer in range(depth):
        x = _layer(
            x, norm_gain[layer].astype(F32)[None, :], win, wout,
            _attention_bias_vector(rel_bias[layer]), cos, sin, dmat, qdec, kdec, bdec, fgain,
            layer=layer, final=(layer == depth - 1))
    return x
```

```python
import functools

import jax
import jax.numpy as jnp
import numpy as np
from jax import lax
from jax.experimental import pallas as pl
from jax.experimental.pallas import tpu as pltpu

D_MODEL = 1024
CHUNK = 64
ATT_HEADS = 8
ATT_HEAD_DIM = 64
ATT_WIDTH = ATT_HEADS * ATT_HEAD_DIM
LEFT_CHUNKS = 8
REL_CLIP = 256
RET_HEADS = 4
RET_HEAD_DIM = 128
RET_WIDTH = RET_HEADS * RET_HEAD_DIM
IN_WIDTH = 4 * ATT_WIDTH + 4 * RET_WIDTH
MIX_WIDTH = ATT_WIDTH + RET_WIDTH
EPS = 1e-6
ROPE_BASE = 10000.0
NEG_INF = -1e30
LOG2E = 1.4426950408889634

LANES = 128
SEQ_BLOCK = 512
HIST = LEFT_CHUNKS * CHUNK
NORM_ROWS = 64
PROJ_ROWS = 256
ATT_QBLOCK = 128
ATT_KBLOCK = HIST + ATT_QBLOCK
ATT_KBLOCKS = ATT_KBLOCK // ATT_QBLOCK
ATT_PAIRS = ATT_HEADS // 2
ATT_LOOKAHEAD = 2
RET_BLOCK = 128
VMEM_LIMIT_BYTES = 56 * 1024 * 1024

BF16 = jnp.bfloat16
F32 = jnp.float32

PIECE_AQ, PIECE_AK, PIECE_AV, PIECE_AG, PIECE_RQ, PIECE_RK, PIECE_RV, PIECE_RG = range(8)
PIECE_ORDER = (PIECE_AQ, PIECE_AG, PIECE_AK, PIECE_RQ, PIECE_AV, PIECE_RK, PIECE_RG, PIECE_RV)


def _dot(a, b):
    return jnp.dot(a, b, preferred_element_type=F32)


def _dot_nt(a, b):
    return lax.dot_general(a, b, (((1,), (1,)), ((), ())), preferred_element_type=F32)


def _dot_tn(a, b):
    return lax.dot_general(a, b, (((0,), (0,)), ((), ())), preferred_element_type=F32)


def _silu(g):
    return g * (1.0 / (1.0 + jnp.exp(-g)))


def _layer_kernel(x_ref, xnext_ref, gain_ref, win_ref, wout_ref, g_ref,
                  cos_ref, sin_ref, cosn_ref, sinn_ref,
                  dmat_ref, qdec_ref, kdec_ref, bdec_ref, fgain_ref,
                  out_ref,
                  bias_s, hn_s, q2_s, kh_s, kn_s, vt_s, vtn_s, ga2_s,
                  rq_s, rqd_s, rk_s, rkd_s, rv_s, gr_s, y_s, state_s,
                  *, final):
    b = pl.program_id(0)
    s = pl.program_id(1)
    first_step = jnp.logical_and(b == 0, s == 0)
    cur = (b * pl.num_programs(1) + s) % 2
    nxt = 1 - cur
    gain = gain_ref[...]
    n_rb = SEQ_BLOCK // RET_BLOCK
    rows = lambda ri: slice(ri * RET_BLOCK, (ri + 1) * RET_BLOCK)
    col = lambda h: slice(h * RET_HEAD_DIM, (h + 1) * RET_HEAD_DIM)

    def rms_norm_rows(src_ref, r0, n):
        xb = src_ref[0, r0:r0 + n, :]
        ms = jnp.mean(xb * xb, axis=-1, keepdims=True)
        hn = (xb * lax.rsqrt(ms + EPS) * gain).astype(BF16)
        hn_s[r0:r0 + n, :] = hn
        return hn

    def rms_norm_into_hn(src_ref):
        for r0 in range(0, SEQ_BLOCK, NORM_ROWS):
            rms_norm_rows(src_ref, r0, NORM_ROWS)


    def project(piece, r0, slot, k_dst, k_row0, vt_dst, vt_col0, cos_t, sin_t):
        rsl = slice(r0, r0 + PROJ_ROWS)
        acc = _dot(hn_s[rsl, :], win_ref[:, piece * ATT_WIDTH:(piece + 1) * ATT_WIDTH])
        if piece == PIECE_AQ:
            q2_s[slot, rsl, :] = (acc * (ATT_HEAD_DIM ** -0.5 * LOG2E)).astype(BF16)
        elif piece == PIECE_AK:
            k_dst[k_row0 + r0:k_row0 + r0 + PROJ_ROWS, :] = acc.astype(BF16)
        elif piece == PIECE_AV:
            vt_dst[:, vt_col0 + r0:vt_col0 + r0 + PROJ_ROWS] = acc.T.astype(BF16)
        elif piece == PIECE_AG:
            ga2_s[slot, rsl, :] = _silu(acc)
        elif piece == PIECE_RV:
            rv_s[rsl, :] = acc.astype(BF16)
        elif piece == PIECE_RG:
            gr_s[rsl, :] = _silu(acc)
        else:
            cos = cos_t[rsl, :]
            sin = sin_t[rsl, :]
            reps = PROJ_ROWS // RET_BLOCK
            for h in range(RET_HEADS):
                t = acc[:, col(h)]
                t = t * cos + pltpu.roll(t, RET_HEAD_DIM // 2, 1) * sin
                if piece == PIECE_RQ:
                    rq_s[rsl, col(h)] = t.astype(BF16)
                    dec = jnp.concatenate([qdec_ref[h]] * reps, axis=0)
                    rqd_s[rsl, col(h)] = (t * dec).astype(BF16)
                else:
                    t = t * (RET_HEAD_DIM ** -0.5)
                    rk_s[rsl, col(h)] = t.astype(BF16)
                    dec = jnp.concatenate([kdec_ref[h]] * reps, axis=0)
                    rkd_s[rsl, col(h)] = (t * dec).astype(BF16)

    proj_chunks = [(piece, r0) for piece in PIECE_ORDER for r0 in range(0, SEQ_BLOCK, PROJ_ROWS)]

    @pl.when(first_step)
    def _():
        row = lax.broadcasted_iota(jnp.int32, (ATT_QBLOCK, ATT_QBLOCK), 0)
        q_chunk = lax.broadcasted_iota(jnp.int32, (ATT_QBLOCK, ATT_QBLOCK), 1) // CHUNK
        for jb in range(ATT_KBLOCKS):
            k_chunk = (row + jb * ATT_QBLOCK) // CHUNK
            in_band = jnp.logical_and(k_chunk >= q_chunk, k_chunk <= q_chunk + LEFT_CHUNKS)
            start = HIST - jb * ATT_QBLOCK
            for h in range(ATT_HEADS):
                window = jnp.broadcast_to(g_ref[h:h + 1, start:start + 2 * ATT_QBLOCK],
                                          (ATT_QBLOCK, 2 * ATT_QBLOCK))
                rolled = pltpu.roll(window, ATT_QBLOCK + 1, 1, stride=1, stride_axis=0)
                bias_s[h, jb] = jnp.where(in_band, rolled[:, :ATT_QBLOCK], NEG_INF)
        for h in range(ATT_HEADS):
            bias_s[h, ATT_KBLOCKS] = jnp.full((ATT_QBLOCK, ATT_QBLOCK), NEG_INF, F32)
        rms_norm_into_hn(x_ref)
        for piece, r0 in proj_chunks:
            project(piece, r0, cur, kh_s, HIST, vt_s, HIST, cos_ref, sin_ref)

    @pl.when(s == 0)
    def _():
        kh_s[0:HIST, :] = jnp.zeros((HIST, ATT_WIDTH), BF16)
        vt_s[:, 0:HIST] = jnp.zeros((ATT_WIDTH, HIST), BF16)
        state_s[...] = jnp.zeros(state_s.shape, F32)

    lane = lax.broadcasted_iota(jnp.int32, (ATT_QBLOCK, LANES), 1)
    low_half = lane < ATT_HEAD_DIM
    items = [(qi, hp) for qi in range(SEQ_BLOCK // ATT_QBLOCK) for hp in range(ATT_PAIRS)]
    items_per_chunk = len(items) // len(proj_chunks)
    assert len(items) == items_per_chunk * len(proj_chunks)
    q_cur = q2_s.at[cur]
    ga_cur = ga2_s.at[cur]

    def scores_t(qi, hp):
        r0 = qi * ATT_QBLOCK
        cs = slice(hp * LANES, (hp + 1) * LANES)
        qp = q_cur[r0:r0 + ATT_QBLOCK, cs]
        zero = jnp.zeros_like(qp)
        qq = jnp.concatenate([jnp.where(low_half, qp, zero), jnp.where(low_half, zero, qp)],
                             axis=0)
        n_before = jnp.where(s == 0, HIST // ATT_QBLOCK - qi, 0)
        blocks = [jnp.where(jb < n_before, ATT_KBLOCKS, jb) for jb in range(ATT_KBLOCKS)]
        bias = jnp.concatenate(
            [jnp.concatenate([bias_s[h, blk] for blk in blocks], axis=0)
             for h in (2 * hp, 2 * hp + 1)], axis=1)
        return _dot_nt(kh_s[r0:r0 + ATT_KBLOCK, cs], qq) + bias

    pending = {it: scores_t(*it) for it in items[:ATT_LOOKAHEAD]}

    ret_items = [(ri, h) for ri in range(n_rb) for h in range(RET_HEADS)]
    kv = {(ri, h): _dot_tn(rkd_s[rows(ri), col(h)], rv_s[rows(ri), col(h)]) for ri, h in ret_items}
    att = {(ri, h): (_dot_nt(rq_s[rows(ri), col(h)], rk_s[rows(ri), col(h)]) * dmat_ref[h]).astype(BF16)
           for ri, h in ret_items}
    state_before = {}
    for h in range(RET_HEADS):
        st = state_s[h]
        for ri in range(n_rb):
            state_before[ri, h] = st.astype(BF16)
            st = st * bdec_ref[h] + kv[ri, h]
        state_s[h] = st
    for ri, h in ret_items:
        lhs = jnp.concatenate([att[ri, h], rqd_s[rows(ri), col(h)]], axis=1)
        rhs = jnp.concatenate([rv_s[rows(ri), col(h)], state_before[ri, h]], axis=0)
        o = _dot(lhs, rhs)
        ms = jnp.mean(o * o, axis=-1, keepdims=True)
        o = o * lax.rsqrt(ms + EPS)
        ycs = slice(ATT_WIDTH + h * RET_HEAD_DIM, ATT_WIDTH + (h + 1) * RET_HEAD_DIM)
        y_s[rows(ri), ycs] = (o * gr_s[rows(ri), col(h)]).astype(BF16)

    rms_norm_into_hn(xnext_ref)

    for n, (qi, hp) in enumerate(items):
        if n + ATT_LOOKAHEAD < len(items):
            ahead = items[n + ATT_LOOKAHEAD]
            pending[ahead] = scores_t(*ahead)
        if n % items_per_chunk == 0:
            project(*proj_chunks[n // items_per_chunk], nxt, kn_s, 0, vtn_s, 0, cosn_ref, sinn_ref)
        sc = pending.pop((qi, hp))
        r0 = qi * ATT_QBLOCK
        cs = slice(hp * LANES, (hp + 1) * LANES)
        m = jnp.max(sc, axis=0, keepdims=True)
        e = jnp.exp2(sc - m)
        l = jnp.sum(e, axis=0, keepdims=True)
        ot = _dot(vt_s[cs, r0:r0 + ATT_KBLOCK], e.astype(BF16))
        rl = 1.0 / l
        o_pair_t = jnp.concatenate([ot[:ATT_HEAD_DIM, :ATT_QBLOCK] * rl[:, :ATT_QBLOCK],
                                    ot[ATT_HEAD_DIM:, ATT_QBLOCK:] * rl[:, ATT_QBLOCK:]],
                                   axis=0)
        y_s[r0:r0 + ATT_QBLOCK, cs] = (
            o_pair_t.T * ga_cur[r0:r0 + ATT_QBLOCK, cs]).astype(BF16)

    res = x_ref[0] + _dot(y_s[...], wout_ref[...])
    if final:
        ms = jnp.mean(res * res, axis=-1, keepdims=True)
        res = res * lax.rsqrt(ms + EPS) * fgain_ref[...]
    out_ref[0] = res

    kh_s[0:HIST, :] = kh_s[SEQ_BLOCK:SEQ_BLOCK + HIST, :]
    kh_s[HIST:HIST + SEQ_BLOCK, :] = kn_s[...]
    vt_s[:, 0:HIST] = vt_s[:, SEQ_BLOCK:SEQ_BLOCK + HIST]
    vt_s[:, HIST:HIST + SEQ_BLOCK] = vtn_s[...]


def _const_spec(shape):
    zeros = (0,) * len(shape)
    return pl.BlockSpec(shape, lambda b, s: zeros)


def _layer_spec(shape, layer):
    zeros = (0,) * (len(shape) - 1)
    return pl.BlockSpec((None,) + tuple(shape[1:]), lambda b, s: (layer,) + zeros)


def _layer(x, gain, win, wout, bias, cos, sin, dmat, qdec, kdec, bdec, fgain, *, layer, final):
    batch, seq, _ = x.shape
    n_seq = seq // SEQ_BLOCK
    grid = (batch, n_seq)

    def next_block(b, s):
        n = jnp.minimum(b * n_seq + s + 1, batch * n_seq - 1)
        return (n // n_seq, n % n_seq, 0)

    blk = pl.BlockSpec((1, SEQ_BLOCK, D_MODEL), lambda b, s: (b, s, 0))
    blk_next = pl.BlockSpec((1, SEQ_BLOCK, D_MODEL), next_block)
    tab = pl.BlockSpec((SEQ_BLOCK, LANES), lambda b, s: (s, 0))
    tab_next = pl.BlockSpec((SEQ_BLOCK, LANES), lambda b, s: next_block(b, s)[1:])
    in_specs = [
        blk, blk_next,
        _const_spec(gain.shape), _layer_spec(win.shape, layer),
        _layer_spec(wout.shape, layer), _const_spec(bias.shape), tab, tab, tab_next, tab_next,
        _const_spec(dmat.shape), _const_spec(qdec.shape), _const_spec(kdec.shape),
        _const_spec(bdec.shape), _const_spec(fgain.shape),
    ]
    scratch = [
        pltpu.VMEM((ATT_HEADS, ATT_KBLOCKS + 1, ATT_QBLOCK, ATT_QBLOCK), F32),
        pltpu.VMEM((SEQ_BLOCK, D_MODEL), BF16),
        pltpu.VMEM((2, SEQ_BLOCK, ATT_WIDTH), BF16),
        pltpu.VMEM((HIST + SEQ_BLOCK, ATT_WIDTH), BF16),
        pltpu.VMEM((SEQ_BLOCK, ATT_WIDTH), BF16),
        pltpu.VMEM((ATT_WIDTH, HIST + SEQ_BLOCK), BF16),
        pltpu.VMEM((ATT_WIDTH, SEQ_BLOCK), BF16),
        pltpu.VMEM((2, SEQ_BLOCK, ATT_WIDTH), F32),
        pltpu.VMEM((SEQ_BLOCK, RET_WIDTH), BF16),
        pltpu.VMEM((SEQ_BLOCK, RET_WIDTH), BF16),
        pltpu.VMEM((SEQ_BLOCK, RET_WIDTH), BF16),
        pltpu.VMEM((SEQ_BLOCK, RET_WIDTH), BF16),
        pltpu.VMEM((SEQ_BLOCK, RET_WIDTH), BF16),
        pltpu.VMEM((SEQ_BLOCK, RET_WIDTH), F32),
        pltpu.VMEM((SEQ_BLOCK, MIX_WIDTH), BF16),
        pltpu.VMEM((RET_HEADS, RET_HEAD_DIM, RET_HEAD_DIM), F32),
    ]
    return pl.pallas_call(
        functools.partial(_layer_kernel, final=final),
        grid=grid,
        in_specs=in_specs,
        out_specs=blk,
        out_shape=jax.ShapeDtypeStruct(x.shape, x.dtype),
        scratch_shapes=scratch,
        compiler_params=pltpu.CompilerParams(
            dimension_semantics=("arbitrary", "arbitrary"),
            vmem_limit_bytes=VMEM_LIMIT_BYTES),
        name="hybrid_layer_final" if final else "hybrid_layer",
    )(x, x, gain, win, wout, bias, cos, sin, cos, sin, dmat, qdec, kdec, bdec, fgain)


def _attention_bias_vector(rel_bias):
    n = ATT_QBLOCK + ATT_KBLOCK - 1
    n_clipped = HIST + ATT_QBLOCK - 1 - REL_CLIP
    rb = rel_bias.astype(F32) * LOG2E
    return jnp.concatenate(
        [rb[:, 2 * REL_CLIP - (n - n_clipped) + 1:],
         jnp.broadcast_to(rb[:, 2 * REL_CLIP:], (ATT_HEADS, n_clipped + 1))], axis=1)


def _retention_tables():
    h = np.arange(RET_HEADS, dtype=np.float32)
    log_g = np.log1p(-np.exp2(-5.0 - h)).astype(np.float32)
    idx = np.arange(RET_BLOCK, dtype=np.float32)
    chunk = np.arange(RET_BLOCK) // CHUNK
    causal = chunk[None, :] <= chunk[:, None]
    dmat = np.exp(np.abs(idx[:, None] - idx[None, :])[None] * log_g[:, None, None])
    dmat = np.where(causal[None], dmat, 0.0)
    qdec = np.exp((idx + 1.0)[None, :] * log_g[:, None])
    kdec = np.exp((RET_BLOCK - 1 - idx)[None, :] * log_g[:, None])
    bdec = np.exp(RET_BLOCK * log_g)
    wide = lambda t: np.broadcast_to(t[:, :, None], (RET_HEADS, RET_BLOCK, LANES))
    bdec = np.broadcast_to(bdec[:, None, None], (RET_HEADS, 1, LANES))
    as_f32 = lambda t: jnp.asarray(np.ascontiguousarray(t, dtype=np.float32))
    return as_f32(dmat), as_f32(wide(qdec)), as_f32(wide(kdec)), as_f32(bdec)


def _rotary_tables(positions):
    half = RET_HEAD_DIM // 2
    inv_freq = 1.0 / (ROPE_BASE ** jnp.linspace(0.0, 1.0, half, dtype=F32))
    ang = positions.astype(F32)[:, None] * inv_freq[None, :]
    cos, sin = jnp.cos(ang), jnp.sin(ang)
    return jnp.concatenate([cos, cos], axis=-1), jnp.concatenate([-sin, sin], axis=-1)


def kernel(x, positions, norm_gain, w_in, w_out, rel_bias, final_gain):
    depth = w_in.shape[0]
    cos, sin = _rotary_tables(positions)
    dmat, qdec, kdec, bdec = _retention_tables()
    fgain = final_gain.astype(F32)[None, :]
    win = w_in.astype(BF16)
    wout = w_out.astype(BF16)
    for layer in range(depth):
        x = _layer(
            x, norm_gain[layer].astype(F32)[None, :], win, wout,
            _attention_bias_vector(rel_bias[layer]), cos, sin, dmat, qdec, kdec, bdec, fgain,
            layer=layer, final=(layer == depth - 1))
    return x
```
